```python
import jax, jax.numpy as jnp
from jax import lax
import numpy as np

D_MODEL = 2048
BATCH = 2
SEQ = 4096
DEPTH = 1
DEC_BATCH = 128
DEC_SEQ = 8
PAST_LEN = 16384
PAGE_SIZE = 128

N_HEADS = 16
Q_RANK = 512
KV_RANK = 512
NOPE_DIM = 128
ROPE_DIM = 64
V_DIM = 128
QK_DIM = NOPE_DIM + ROPE_DIM
ROPE_THETA = 10000.0
ATTN_SCALE = QK_DIM ** -0.5
Q_BLOCK = 128
CONV_WIDTH = D_MODEL
CONV_K = 3
D_FF = ((-(-(8 * D_MODEL) // 3) + 255) // 256) * 256
N_BRANCHES = 2
IN_COLS = Q_RANK + KV_RANK + ROPE_DIM + 3 * CONV_WIDTH + N_BRANCHES * D_MODEL
EPS = 1e-6
NEG_INF = -1e30

kernel_name = 'mla_shortconv_gated_parallel_decoder_step'


def rmsnorm(x, g):
    xf = x.astype(jnp.float32)
    y = xf * lax.rsqrt(jnp.mean(xf * xf, axis=-1, keepdims=True) + EPS)
    return (y * g.astype(jnp.float32)).astype(x.dtype)


def inv_rms(x):
    xf = x.astype(jnp.float32)
    return lax.rsqrt(jnp.mean(xf * xf, axis=-1) + EPS)


def apply_rope(x, pos):
    half = ROPE_DIM // 2
    freqs = ROPE_THETA ** (-jnp.arange(half, dtype=jnp.float32) / half)
    ang = pos.astype(jnp.float32)[..., None] * freqs
    c, s = jnp.cos(ang), jnp.sin(ang)
    xf = x.astype(jnp.float32)
    x1, x2 = xf[..., :half], xf[..., half:]
    return jnp.concatenate([x1 * c - x2 * s, x2 * c + x1 * s], axis=-1).astype(x.dtype)


def mixer_inputs(x, pos, g_attn, w_in, b_gate, g_q_lat, g_kv_lat, w_uq, g_q_nope, g_q_rope, g_k_rope, w_uk):
    n = rmsnorm(x, g_attn)
    proj = jnp.einsum('btd,dc->btc', n, w_in)
    i0 = Q_RANK
    i1 = i0 + KV_RANK
    i2 = i1 + ROPE_DIM
    i3 = i2 + CONV_WIDTH
    i4 = i3 + CONV_WIDTH
    i5 = i4 + CONV_WIDTH
    c_q, c_kv, k_r, conv_b, conv_c, conv_x, gate_logits = jnp.split(proj, [i0, i1, i2, i3, i4, i5], axis=-1)
    c_q = rmsnorm(c_q, g_q_lat)
    q = jnp.einsum('btr,rhd->bthd', c_q, w_uq)
    q_nope = rmsnorm(q[..., :NOPE_DIM], g_q_nope)
    q_rope = apply_rope(rmsnorm(q[..., NOPE_DIM:], g_q_rope), pos[:, None])
    c_kv = rmsnorm(c_kv, g_kv_lat)
    k_rope = apply_rope(rmsnorm(k_r, g_k_rope), pos)
    k_nope_raw = jnp.einsum('btr,rhd->bthd', c_kv, w_uk)
    k_inv = inv_rms(k_nope_raw).astype(x.dtype)
    u = conv_c * conv_x
    gates = jax.nn.sigmoid(gate_logits + b_gate)
    return q_nope, q_rope, c_kv, k_rope, k_nope_raw, k_inv, u, conv_b, gates


def short_conv(u_ext, w_conv, t_len):
    return sum(w_conv[j] * u_ext[:, j:j + t_len] for j in range(CONV_K))


def merge_and_ffn(x, attn_out, conv_out, gates, w_o, g_ffn, w_gate_up, w_down):
    g_a, g_c = jnp.split(gates, N_BRANCHES, axis=-1)
    mixed = g_a * attn_out + g_c * conv_out
    h = x + jnp.einsum('btd,de->bte', mixed, w_o)
    n2 = rmsnorm(h, g_ffn)
    gt, up = jnp.split(jnp.einsum('btd,df->btf', n2, w_gate_up), 2, axis=-1)
    return h + jnp.einsum('btf,fd->btd', jax.nn.silu(gt) * up, w_down)


def prompt_attention(q_nope, q_rope, k_nope, k_rope, v):
    b, s = q_nope.shape[0], q_nope.shape[1]
    nb = s // Q_BLOCK
    qn = q_nope.reshape(b, nb, Q_BLOCK, N_HEADS, NOPE_DIM).transpose(1, 0, 2, 3, 4)
    qr = q_rope.reshape(b, nb, Q_BLOCK, N_HEADS, ROPE_DIM).transpose(1, 0, 2, 3, 4)
    kpos = jnp.arange(s)

    def block(args):
        i, qn_b, qr_b = args
        sc = (jnp.einsum('bqhd,bkhd->bhqk', qn_b, k_nope).astype(jnp.float32)
              + jnp.einsum('bqhr,bkr->bhqk', qr_b, k_rope).astype(jnp.float32)) * ATTN_SCALE
        qpos = i * Q_BLOCK + jnp.arange(Q_BLOCK)
        sc = jnp.where(kpos[None, :] <= qpos[:, None], sc, NEG_INF)
        p = jax.nn.softmax(sc, axis=-1).astype(v.dtype)
        return jnp.einsum('bhqk,bkhd->bqhd', p, v)

    out = lax.map(block, (jnp.arange(nb), qn, qr))
    return out.transpose(1, 0, 2, 3, 4).reshape(b, s, N_HEADS * V_DIM)


def setup_inputs(seed: int = 0) -> dict:
    key = jax.random.key(seed)
    ks = jax.random.split(key, 32)
    n_pages = PAST_LEN // PAGE_SIZE
    n_pool = (DEC_BATCH * n_pages * 5) // 4
    f32 = jnp.float32

    def nrm(k, shape, scale):
        return jax.random.normal(k, shape, f32) * scale

    def gain(k, n):
        return 1.0 + 0.02 * jax.random.normal(k, (n,), f32)

    page_table = jax.random.permutation(ks[7], n_pool)[:DEC_BATCH * n_pages].reshape(DEC_BATCH, n_pages).astype(jnp.int32)
    return {
        'x_prompt': nrm(ks[0], (BATCH, SEQ, D_MODEL), 1.0),
        'x_sample': nrm(ks[1], (DEC_BATCH, DEC_SEQ, D_MODEL), 1.0),
        'cache_kv_latent': nrm(ks[2], (n_pool, PAGE_SIZE, KV_RANK), 1.0),
        'cache_k_rope': nrm(ks[3], (n_pool, PAGE_SIZE, ROPE_DIM), 1.0),
        'cache_k_inv_rms': jax.random.uniform(ks[4], (n_pool, PAGE_SIZE, N_HEADS), f32, 0.8, 1.25),
        'state_conv': nrm(ks[5], (DEC_BATCH, CONV_K - 1, CONV_WIDTH), 1.0),
        'page_table': page_table,
        'g_attn': gain(ks[8], D_MODEL),
        'w_in': nrm(ks[9], (D_MODEL, IN_COLS), D_MODEL ** -0.5),
        'b_gate': nrm(ks[10], (N_BRANCHES * D_MODEL,), 0.02),
        'g_q_lat': gain(ks[11], Q_RANK),
        'g_kv_lat': gain(ks[12], KV_RANK),
        'w_uq': nrm(ks[13], (Q_RANK, N_HEADS, QK_DIM), Q_RANK ** -0.5),
        'g_q_nope': gain(ks[14], NOPE_DIM),
        'g_q_rope': gain(ks[15], ROPE_DIM),
        'g_k_nope': gain(ks[16], NOPE_DIM),
        'g_k_rope': gain(ks[17], ROPE_DIM),
        'w_uk': nrm(ks[18], (KV_RANK, N_HEADS, NOPE_DIM), KV_RANK ** -0.5),
        'w_uv': nrm(ks[19], (KV_RANK, N_HEADS, V_DIM), KV_RANK ** -0.5),
        'w_conv': nrm(ks[20], (CONV_K, CONV_WIDTH), CONV_K ** -0.5),
        'w_o': nrm(ks[21], (D_MODEL, D_MODEL), D_MODEL ** -0.5),
        'g_ffn': gain(ks[22], D_MODEL),
        'w_gate_up': nrm(ks[23], (D_MODEL, 2 * D_FF), D_MODEL ** -0.5),
        'w_down': nrm(ks[24], (D_FF, D_MODEL), D_FF ** -0.5),
    }


def reference(x_prompt, x_sample, cache_kv_latent, cache_k_rope, cache_k_inv_rms, state_conv, page_table,
              g_attn, w_in, b_gate, g_q_lat, g_kv_lat, w_uq, g_q_nope, g_q_rope, g_k_nope, g_k_rope,
              w_uk, w_uv, w_conv, w_o, g_ffn, w_gate_up, w_down):
    seq = x_prompt.shape[1]
    t_dec = x_sample.shape[1]
    past_len = page_table.shape[1] * PAGE_SIZE

    pos_p = jnp.arange(seq)
    q_nope, q_rope, c_kv_p, k_rope_p, k_nope_raw, k_inv_p, u_p, conv_b_p, gates_p = mixer_inputs(
        x_prompt, pos_p, g_attn, w_in, b_gate, g_q_lat, g_kv_lat, w_uq, g_q_nope, g_q_rope, g_k_rope, w_uk)
    k_nope = k_nope_raw * k_inv_p[..., None] * g_k_nope
    v_p = jnp.einsum('btr,rhd->bthd', c_kv_p, w_uv)
    attn_p = prompt_attention(q_nope, q_rope, k_nope, k_rope_p, v_p)
    u_ext_p = jnp.concatenate([jnp.zeros((u_p.shape[0], CONV_K - 1, CONV_WIDTH), u_p.dtype), u_p], axis=1)
    conv_p = conv_b_p * short_conv(u_ext_p, w_conv, seq)
    y_prompt = merge_and_ffn(x_prompt, attn_p, conv_p, gates_p, w_o, g_ffn, w_gate_up, w_down)
    conv_state_prompt = u_ext_p[:, -(CONV_K - 1):]

    pos_s = past_len + jnp.arange(t_dec)
    q_nope_s, q_rope_s, c_kv_s, k_rope_s, _, k_inv_s, u_s, conv_b_s, gates_s = mixer_inputs(
        x_sample, pos_s, g_attn, w_in, b_gate, g_q_lat, g_kv_lat, w_uq, g_q_nope, g_q_rope, g_k_rope, w_uk)
    q_abs = jnp.einsum('bthd,rhd->bthr', q_nope_s * g_k_nope, w_uk)
    kidx = jnp.arange(past_len + t_dec)
    tq = jnp.arange(t_dec)
    dec_mask = (kidx[None, :] < past_len) | ((kidx[None, :] - past_len) <= tq[:, None])

    def one_seq(args):
        pages, qa, qr, c_new, kr_new, inv_new = args
        c_all = jnp.concatenate([cache_kv_latent[pages].reshape(-1, KV_RANK), c_new], axis=0)
        kr_all = jnp.concatenate([cache_k_rope[pages].reshape(-1, ROPE_DIM), kr_new], axis=0)
        inv_all = jnp.concatenate([cache_k_inv_rms[pages].reshape(-1, N_HEADS), inv_new], axis=0)
        sc = (jnp.einsum('thr,kr->htk', qa, c_all).astype(jnp.float32) * inv_all.T.astype(jnp.float32)[:, None, :]
              + jnp.einsum('thr,kr->htk', qr, kr_all).astype(jnp.float32)) * ATTN_SCALE
        sc = jnp.where(dec_mask[None], sc, NEG_INF)
        p = jax.nn.softmax(sc, axis=-1).astype(c_all.dtype)
        return jnp.einsum('htk,kr->thr', p, c_all)

    o_lat = lax.map(one_seq, (page_table, q_abs, q_rope_s, c_kv_s, k_rope_s, k_inv_s))
    attn_s = jnp.einsum('bthr,rhd->bthd', o_lat, w_uv).reshape(x_sample.shape[0], t_dec, N_HEADS * V_DIM)
    u_ext_s = jnp.concatenate([state_conv.astype(u_s.dtype), u_s], axis=1)
    conv_s = conv_b_s * short_conv(u_ext_s, w_conv, t_dec)
    y_sample = merge_and_ffn(x_sample, attn_s, conv_s, gates_s, w_o, g_ffn, w_gate_up, w_down)
    conv_state_sample = u_ext_s[:, -(CONV_K - 1):]

    return (y_prompt, y_sample, c_kv_p, k_rope_p, k_inv_p, conv_state_prompt,
            c_kv_s, k_rope_s, k_inv_s, conv_state_sample)
```

```python
import functools

import jax
import jax.numpy as jnp
from jax import lax
from jax.experimental import pallas as pl
from jax.experimental.pallas import tpu as pltpu

N_HEADS = 16
Q_RANK = 512
KV_RANK = 512
NOPE_DIM = 128
ROPE_DIM = 64
HALF_ROPE = ROPE_DIM // 2
V_DIM = 128
QK_DIM = NOPE_DIM + ROPE_DIM
QK_PAD = 256
ROPE_THETA = 10000.0
ATTN_SCALE = QK_DIM ** -0.5
PAGE_SIZE = 128
CONV_K = 3
EPS = 1e-6
NEG_INF = -1e30
LANES = 128
SUBLANES = 8
VMEM_LIMIT = 56 * 1024 * 1024

F32 = jnp.float32
BF16 = jnp.bfloat16
NT_DIMS = (((1,), (1,)), ((), ()))


def _params(*sem):
    return pltpu.CompilerParams(dimension_semantics=sem, vmem_limit_bytes=VMEM_LIMIT)


def _unit_rms(xf):
    return xf * lax.rsqrt(jnp.mean(xf * xf, axis=-1, keepdims=True) + EPS)


def _dot(a, b):
    return jnp.dot(a, b, preferred_element_type=F32)


def _dot_nt(a, b):
    return lax.dot_general(a, b, NT_DIMS, preferred_element_type=F32)


def _rest_kernel(x_ref, g_ref, wb_ref, wc_ref, wx_ref, wga_ref, wgc_ref, bga_ref, bgc_ref,
                 b_out, u_out, ga_out, gc_out, n_scr):
    @pl.when(pl.program_id(1) == 0)
    def _():
        n_scr[...] = (_unit_rms(x_ref[...]) * g_ref[...]).astype(BF16)

    n = n_scr[...]
    b_out[...] = _dot(n, wb_ref[...]).astype(BF16)
    u_out[...] = _dot(n, wc_ref[...]) * _dot(n, wx_ref[...])
    ga_out[...] = jax.nn.sigmoid(_dot(n, wga_ref[...]) + bga_ref[...]).astype(BF16)
    gc_out[...] = jax.nn.sigmoid(_dot(n, wgc_ref[...]) + bgc_ref[...]).astype(BF16)


def _rest_proj(x2, g_attn, w_b, w_c, w_x, w_ga, w_gc, b_ga, b_gc, *, bm, bn):
    m, d = x2.shape
    n = w_b.shape[1]
    row = lambda i, j: (i, 0)
    col = lambda i, j: (0, j)
    tile = lambda i, j: (i, j)
    w_spec = pl.BlockSpec((d, bn), col)
    v_spec = pl.BlockSpec((1, bn), col)
    o_spec = pl.BlockSpec((bm, bn), tile)
    return pl.pallas_call(
        _rest_kernel,
        grid=(m // bm, n // bn),
        in_specs=[pl.BlockSpec((bm, d), row), pl.BlockSpec((1, d), lambda i, j: (0, 0)),
                  w_spec, w_spec, w_spec, w_spec, w_spec, v_spec, v_spec],
        out_specs=[o_spec, o_spec, o_spec, o_spec],
        out_shape=[jax.ShapeDtypeStruct((m, n), BF16), jax.ShapeDtypeStruct((m, n), F32),
                   jax.ShapeDtypeStruct((m, n), BF16), jax.ShapeDtypeStruct((m, n), BF16)],
        scratch_shapes=[pltpu.VMEM((bm, d), BF16)],
        compiler_params=_params("arbitrary", "arbitrary"),
    )(x2, g_attn, w_b, w_c, w_x, w_ga, w_gc, b_ga, b_gc)


def _latent_kernel(prompt, x_ref, cos_ref, sin_ref, g_attn, g_q_lat, g_kv_lat, g_kr, g_q_nope, g_qr_a,
                   g_qr_b, g_k_nope, w_lat, w_uq, w_uk, w_aux, *outs):
    if prompt:
        qcat_out, kcat_out, vt_out, ckv_out, krope_out, kinv_out = outs
    else:
        qa_out, qr_out, ckv_out, krope_out, kinv_out = outs
    bm = x_ref.shape[1]
    cos_t = cos_ref[...]
    sin_t = sin_ref[...]
    lane = lax.broadcasted_iota(jnp.int32, (bm, LANES), 1)
    low_half = lane < ROPE_DIM

    n = (_unit_rms(x_ref[0]) * g_attn[...]).astype(BF16)
    lat = _dot(n, w_lat[...])
    cq = (_unit_rms(lat[:, :Q_RANK]) * g_q_lat[...]).astype(BF16)
    ckv = _unit_rms(lat[:, Q_RANK:Q_RANK + KV_RANK]) * g_kv_lat[...]
    ckv_out[0] = ckv
    ckv_b = ckv.astype(BF16)

    krp = lat[:, Q_RANK + KV_RANK:]
    kr_ms = jnp.sum(jnp.where(low_half, krp * krp, 0.0), axis=-1, keepdims=True) * (1.0 / ROPE_DIM)
    krp = krp * lax.rsqrt(kr_ms + EPS) * g_kr[...]
    krt = krp * jnp.where(low_half, cos_t, sin_t)
    kr2 = krt + pltpu.roll(krt, ROPE_DIM, axis=1)
    krope_out[0] = kr2[:, :ROPE_DIM]

    kraw = _dot(ckv_b, w_uk[...])
    kinv = jnp.zeros((bm, LANES), F32)
    for h in range(N_HEADS):
        kh = kraw[:, h * NOPE_DIM:(h + 1) * NOPE_DIM]
        inv_h = lax.rsqrt(jnp.mean(kh * kh, axis=-1, keepdims=True) + EPS)
        kinv = jnp.where(lane == h, inv_h, kinv)
        if prompt:
            kcat_out[0, h, :, :NOPE_DIM] = (kh * inv_h * g_k_nope[...]).astype(BF16)
            kcat_out[0, h, :, NOPE_DIM:] = kr2.astype(BF16)
    kinv_out[0] = kinv[:, :N_HEADS]

    if prompt:
        vt = _dot_nt(w_aux[...], ckv_b)
        vt_out[0, :, 0] = vt.reshape(N_HEADS, V_DIM, bm).astype(BF16)

    q = _dot(cq, w_uq[...])
    rope_a0 = N_HEADS * NOPE_DIM
    rope_b0 = rope_a0 + N_HEADS * ROPE_DIM
    for h in range(N_HEADS):
        qh = _unit_rms(q[:, h * NOPE_DIM:(h + 1) * NOPE_DIM]) * (g_q_nope[...] * ATTN_SCALE)
        if prompt:
            qcat_out[0, h, :, :NOPE_DIM] = qh.astype(BF16)
        else:
            qa_out[0, h] = _dot((qh * g_k_nope[...]).astype(BF16), w_aux[h])
    for p in range(N_HEADS // 2):
        qa_ = q[:, rope_a0 + p * LANES: rope_a0 + (p + 1) * LANES]
        qb_ = q[:, rope_b0 + p * LANES: rope_b0 + (p + 1) * LANES]
        sq = qa_ * qa_
        ms_lo = jnp.sum(jnp.where(low_half, sq, 0.0), axis=-1, keepdims=True) * (1.0 / ROPE_DIM)
        ms_hi = jnp.sum(jnp.where(low_half, 0.0, sq), axis=-1, keepdims=True) * (1.0 / ROPE_DIM)
        inv = jnp.where(low_half, lax.rsqrt(ms_lo + EPS), lax.rsqrt(ms_hi + EPS)) * ATTN_SCALE
        rot = (qa_ * g_qr_a[...] * cos_t + qb_ * g_qr_b[...] * sin_t) * inv
        even = jnp.where(low_half, rot, 0.0)
        odd = jnp.where(low_half, 0.0, rot)
        if prompt:
            qcat_out[0, 2 * p, :, NOPE_DIM:] = even.astype(BF16)
            qcat_out[0, 2 * p + 1, :, NOPE_DIM:] = odd.astype(BF16)
        else:
            qr_out[0, 2 * p] = even
            qr_out[0, 2 * p + 1] = odd


def _latent_proj(prompt, x3, cos_t, sin_t, gains, w_lat, w_uq, w_uk, w_aux, *, bm):
    g, r, d = x3.shape
    nblk = r // bm
    tab_blocks = cos_t.shape[0] // bm
    const2 = lambda a: pl.BlockSpec(a.shape, lambda b, i: (0, 0), pipeline_mode=pl.Buffered(1))
    const3 = lambda a: pl.BlockSpec(a.shape, lambda b, i: (0, 0, 0), pipeline_mode=pl.Buffered(1))
    tab_spec = pl.BlockSpec((bm, LANES), lambda b, i: (i % tab_blocks, 0))
    row3 = lambda w: pl.BlockSpec((1, bm, w), lambda b, i: (b, i, 0))
    head4 = lambda w: pl.BlockSpec((1, N_HEADS, bm, w), lambda b, i: (b, 0, i, 0))
    if prompt:
        out_specs = [head4(QK_PAD), head4(QK_PAD),
                     pl.BlockSpec((1, N_HEADS, 1, V_DIM, bm), lambda b, i: (b, 0, i, 0, 0)),
                     row3(KV_RANK), row3(ROPE_DIM), row3(N_HEADS)]
        out_shape = [jax.ShapeDtypeStruct((g, N_HEADS, r, QK_PAD), BF16),
                     jax.ShapeDtypeStruct((g, N_HEADS, r, QK_PAD), BF16),
                     jax.ShapeDtypeStruct((g, N_HEADS, nblk, V_DIM, bm), BF16)]
    else:
        out_specs = [head4(KV_RANK), head4(LANES), row3(KV_RANK), row3(ROPE_DIM), row3(N_HEADS)]
        out_shape = [jax.ShapeDtypeStruct((g, N_HEADS, r, KV_RANK), F32),
                     jax.ShapeDtypeStruct((g, N_HEADS, r, LANES), F32)]
    out_shape += [jax.ShapeDtypeStruct((g, r, KV_RANK), F32), jax.ShapeDtypeStruct((g, r, ROPE_DIM), F32),
                  jax.ShapeDtypeStruct((g, r, N_HEADS), F32)]
    aux_spec = const2(w_aux) if prompt else const3(w_aux)
    return pl.pallas_call(
        functools.partial(_latent_kernel, prompt),
        grid=(g, nblk),
        in_specs=[pl.BlockSpec((1, bm, d), lambda b, i: (b, i, 0)), tab_spec, tab_spec]
                 + [const2(a) for a in gains] + [const2(w_lat), const2(w_uq), const2(w_uk), aux_spec],
        out_specs=out_specs,
        out_shape=out_shape,
        compiler_params=_params("arbitrary", "arbitrary"),
    )(x3, cos_t, sin_t, *gains, w_lat, w_uq, w_uk, w_aux)


def _prompt_attn_kernel(q_ref, k_ref, vt_ref, o_ref, m_scr, l_scr, acc_scr, *, tq, vb):
    qi = pl.program_id(2)
    q = q_ref[0, 0]
    m_scr[...] = jnp.full(m_scr.shape, NEG_INF, F32)
    l_scr[...] = jnp.zeros(l_scr.shape, F32)
    acc_scr[...] = jnp.zeros(acc_scr.shape, F32)

    def step(j, masked):
        k = k_ref[0, 0, pl.ds(pl.multiple_of(j * tq, tq), tq), :]
        st = _dot_nt(k, q)
        if masked:
            kpos = lax.broadcasted_iota(jnp.int32, (tq, tq), 0)
            qpos = lax.broadcasted_iota(jnp.int32, (tq, tq), 1)
            st = jnp.where(kpos <= qpos, st, NEG_INF)
        m_prev = m_scr[...]
        m_new = jnp.maximum(m_prev, jnp.max(st, axis=0, keepdims=True))
        p = jnp.exp(st - m_new)
        alpha = jnp.exp(m_prev - m_new)
        l_scr[...] = alpha * l_scr[...] + jnp.sum(p, axis=0, keepdims=True)
        pb = p.astype(BF16)
        pv = _dot(vt_ref[0, 0, j * (tq // vb)], pb[:vb])
        for s in range(1, tq // vb):
            pv += _dot(vt_ref[0, 0, j * (tq // vb) + s], pb[s * vb:(s + 1) * vb])
        acc_scr[...] = alpha * acc_scr[...] + pv
        m_scr[...] = m_new

    def body(j, carry):
        step(j, False)
        return carry

    lax.fori_loop(0, qi, body, 0)
    step(qi, True)
    o = acc_scr[...] * (1.0 / l_scr[...])
    o_ref[0] = o.T.astype(BF16)


def _prompt_attn(qcat, kcat, vt, *, tq):
    b, h, s, w = qcat.shape
    nvb, vb = vt.shape[2], vt.shape[4]
    return pl.pallas_call(
        functools.partial(_prompt_attn_kernel, tq=tq, vb=vb),
        grid=(b, h, s // tq),
        in_specs=[pl.BlockSpec((1, 1, tq, w), lambda bi, hi, qi: (bi, hi, qi, 0)),
                  pl.BlockSpec((1, 1, s, w), lambda bi, hi, qi: (bi, hi, 0, 0)),
                  pl.BlockSpec((1, 1, nvb, V_DIM, vb), lambda bi, hi, qi: (bi, hi, 0, 0, 0))],
        out_specs=pl.BlockSpec((1, tq, V_DIM), lambda bi, hi, qi: (bi, qi, hi)),
        out_shape=jax.ShapeDtypeStruct((b, s, h * V_DIM), BF16),
        scratch_shapes=[pltpu.VMEM((1, tq), F32), pltpu.VMEM((1, tq), F32), pltpu.VMEM((V_DIM, tq), F32)],
        compiler_params=_params("arbitrary", "arbitrary", "arbitrary"),
    )(qcat, kcat, vt)


def _expand_heads(inv_t):
    k = inv_t.shape[1]
    return jnp.broadcast_to(inv_t[:, None, :], (N_HEADS, SUBLANES, k)).reshape(N_HEADS * SUBLANES, k)


def _decode_kernel(pages_per_step, pt_ref, qa_ref, qr_ref, cnew_ref, krnew_ref, gnew_ref, *rest):
    pps = pages_per_step
    c_refs, kr_refs, inv_refs = rest[:pps], rest[pps:2 * pps], rest[2 * pps:3 * pps]
    o_ref, qa_scr, qr_scr, cb_scr, s_scr, m_scr, l_scr, acc_scr = rest[3 * pps:]
    j = pl.program_id(1)
    rows = N_HEADS * SUBLANES

    @pl.when(j == 0)
    def _():
        qa_scr[...] = qa_ref[0].reshape(rows, KV_RANK).astype(BF16)
        qr = qr_ref[0].reshape(rows, LANES)
        qr_scr[...] = (qr[:, :ROPE_DIM] + qr[:, ROPE_DIM:]).astype(BF16)
        m_scr[...] = jnp.full(m_scr.shape, NEG_INF, F32)
        l_scr[...] = jnp.zeros(l_scr.shape, F32)
        acc_scr[...] = jnp.zeros(acc_scr.shape, F32)

    def update(s, values_b):
        m_prev = m_scr[...]
        m_new = jnp.maximum(m_prev, jnp.max(s, axis=-1, keepdims=True))
        p = jnp.exp(s - m_new)
        alpha = jnp.exp(m_prev - m_new)
        l_scr[...] = alpha * l_scr[...] + jnp.sum(p, axis=-1, keepdims=True)
        acc_scr[...] = alpha * acc_scr[...] + _dot(p.astype(BF16), values_b)
        m_scr[...] = m_new

    qa = qa_scr[...]
    qr = qr_scr[...]
    for p in range(pps):
        cb = c_refs[p][0].astype(BF16)
        cb_scr[p * PAGE_SIZE:(p + 1) * PAGE_SIZE, :] = cb
        s_nope = _dot_nt(qa, cb) * _expand_heads(inv_refs[p][0])
        s_rope = _dot_nt(qr, kr_refs[p][0].astype(BF16))
        s_scr[:, p * PAGE_SIZE:(p + 1) * PAGE_SIZE] = s_nope + s_rope
    update(s_scr[...], cb_scr[...])

    @pl.when(j == pl.num_programs(1) - 1)
    def _():
        t_new = cnew_ref.shape[1]
        pad = lambda a: jnp.concatenate([a, jnp.zeros((PAGE_SIZE - t_new, a.shape[1]), F32)], axis=0).astype(BF16)
        cn = pad(cnew_ref[0])
        s = _dot_nt(qa, cn) * gnew_ref[0] + _dot_nt(qr, pad(krnew_ref[0]))
        tok = lax.broadcasted_iota(jnp.int32, (rows, PAGE_SIZE), 0) % SUBLANES
        key = lax.broadcasted_iota(jnp.int32, (rows, PAGE_SIZE), 1)
        update(jnp.where(key <= tok, s, NEG_INF), cn)
        o_ref[0] = acc_scr[...] * (1.0 / l_scr[...])


def _decode_attn(page_table, qa, qr, c_new, kr_new, g_new, cache_c, cache_kr, cache_inv_t, *, pages_per_step):
    nseq, npages = page_table.shape
    t_new = c_new.shape[1]
    assert t_new == SUBLANES and npages % pages_per_step == 0
    pps = pages_per_step
    rows = N_HEADS * t_new
    keys = pps * PAGE_SIZE

    def page_spec(shape, p):
        return pl.BlockSpec((1,) + shape, lambda b, j, pt: (pt[b, j * pps + p], 0, 0))

    in_specs = [pl.BlockSpec((1, N_HEADS, t_new, KV_RANK), lambda b, j, pt: (0, 0, b, 0)),
                pl.BlockSpec((1, N_HEADS, t_new, LANES), lambda b, j, pt: (0, 0, b, 0)),
                pl.BlockSpec((1, t_new, KV_RANK), lambda b, j, pt: (b, 0, 0)),
                pl.BlockSpec((1, t_new, ROPE_DIM), lambda b, j, pt: (b, 0, 0)),
                pl.BlockSpec((1, rows, PAGE_SIZE), lambda b, j, pt: (b, 0, 0))]
    in_specs += [page_spec((PAGE_SIZE, KV_RANK), p) for p in range(pps)]
    in_specs += [page_spec((PAGE_SIZE, ROPE_DIM), p) for p in range(pps)]
    in_specs += [page_spec((N_HEADS, PAGE_SIZE), p) for p in range(pps)]
    grid_spec = pltpu.PrefetchScalarGridSpec(
        num_scalar_prefetch=1,
        grid=(nseq, npages // pps),
        in_specs=in_specs,
        out_specs=pl.BlockSpec((1, rows, KV_RANK), lambda b, j, pt: (b, 0, 0)),
        scratch_shapes=[pltpu.VMEM((rows, KV_RANK), BF16), pltpu.VMEM((rows, ROPE_DIM), BF16),
                        pltpu.VMEM((keys, KV_RANK), BF16), pltpu.VMEM((rows, keys), F32),
                        pltpu.VMEM((rows, 1), F32), pltpu.VMEM((rows, 1), F32), pltpu.VMEM((rows, KV_RANK), F32)],
    )
    return pl.pallas_call(
        functools.partial(_decode_kernel, pps),
        grid_spec=grid_spec,
        out_shape=jax.ShapeDtypeStruct((nseq, rows, KV_RANK), F32),
        compiler_params=_params("arbitrary", "arbitrary"),
    )(page_table, qa, qr, c_new, kr_new, g_new, *([cache_c] * pps), *([cache_kr] * pps), *([cache_inv_t] * pps))


def _uv_kernel(o_ref, w_ref, out_ref):
    nseq, _, t, r = o_ref.shape
    o = o_ref[...].reshape(nseq * t, r).astype(BF16)
    out_ref[...] = _dot(o, w_ref[0]).astype(BF16)


def _uv_proj(o_lat4, w_uv_h):
    nseq, h, t, r = o_lat4.shape
    return pl.pallas_call(
        _uv_kernel,
        grid=(h,),
        in_specs=[pl.BlockSpec((nseq, 1, t, r), lambda i: (0, i, 0, 0)),
                  pl.BlockSpec((1, r, V_DIM), lambda i: (i, 0, 0))],
        out_specs=pl.BlockSpec((nseq * t, V_DIM), lambda i: (0, i)),
        out_shape=jax.ShapeDtypeStruct((nseq * t, h * V_DIM), BF16),
        compiler_params=_params("arbitrary"),
    )(o_lat4, w_uv_h)


def _merge_kernel(short_seq, x_ref, attn_ref, b_ref, u_ref, prev_ref, ga_ref, gc_ref, wconv_ref, wo_ref,
                  gffn_ref, h_out, n2_out):
    u = u_ref[...]
    bm, c = u.shape
    if short_seq:
        nseq = bm // SUBLANES
        u3 = u.reshape(nseq, SUBLANES, c)
        tok = lax.broadcasted_iota(jnp.int32, (nseq, SUBLANES, c), 1)
        st = prev_ref[...]
        s0, s1 = st[:, 0:1, :], st[:, 1:2, :]
        u1 = jnp.where(tok == 0, s1, pltpu.roll(u3, 1, axis=1))
        u2 = jnp.where(tok == 0, s0, jnp.where(tok == 1, s1, pltpu.roll(u3, 2, axis=1)))
        u1 = u1.reshape(bm, c)
        u2 = u2.reshape(bm, c)
    else:
        first = pl.program_id(1) == 0
        halo = jnp.where(first, 0.0, prev_ref[0])
        row = lax.broadcasted_iota(jnp.int32, (bm, c), 0)
        h1, h2 = halo[SUBLANES - 1:SUBLANES, :], halo[SUBLANES - 2:SUBLANES - 1, :]
        u1 = jnp.where(row == 0, h1, pltpu.roll(u, 1, axis=0))
        u2 = jnp.where(row == 0, h2, jnp.where(row == 1, h1, pltpu.roll(u, 2, axis=0)))
    wc = wconv_ref[...]
    conv = wc[0:1, :] * u2 + wc[1:2, :] * u1 + wc[2:3, :] * u
    mixed = ga_ref[...].astype(F32) * attn_ref[...].astype(F32) + gc_ref[...].astype(F32) * (b_ref[...].astype(F32) * conv)
    h = x_ref[...] + _dot(mixed.astype(BF16), wo_ref[...])
    h_out[...] = h
    n2_out[...] = (_unit_rms(h) * gffn_ref[...]).astype(BF16)


def _merge(short_seq, x3, attn3, b3, u3, prev, ga3, gc3, w_conv, w_o, g_ffn, *, bm):
    g, r, d = x3.shape
    row = pl.BlockSpec((None, bm, d), lambda b, i: (b, i, 0))
    if short_seq:
        prev_spec = pl.BlockSpec((bm // SUBLANES, CONV_K - 1, d), lambda b, i: (i, 0, 0))
    else:
        per = bm // SUBLANES
        prev_spec = pl.BlockSpec((1, SUBLANES, d), lambda b, i: (b, jnp.maximum(i * per - 1, 0), 0))
    return pl.pallas_call(
        functools.partial(_merge_kernel, short_seq),
        grid=(g, r // bm),
        in_specs=[row, row, row, row, prev_spec, row, row,
                  pl.BlockSpec(w_conv.shape, lambda b, i: (0, 0)),
                  pl.BlockSpec(w_o.shape, lambda b, i: (0, 0), pipeline_mode=pl.Buffered(1)),
                  pl.BlockSpec(g_ffn.shape, lambda b, i: (0, 0))],
        out_specs=[row, row],
        out_shape=[jax.ShapeDtypeStruct((g, r, d), F32), jax.ShapeDtypeStruct((g, r, d), BF16)],
        compiler_params=_params("arbitrary", "arbitrary"),
    )(x3, attn3, b3, u3, prev, ga3, gc3, w_conv, w_o, g_ffn)


def _ffn_kernel(n2_ref, h_ref, wg_ref, wu_ref, wd_ref, y_ref, acc_scr):
    f = pl.program_id(1)

    @pl.when(f == 0)
    def _():
        acc_scr[...] = h_ref[...]

    n2 = n2_ref[...]
    act = (jax.nn.silu(_dot(n2, wg_ref[...])) * _dot(n2, wu_ref[...])).astype(BF16)
    acc_scr[...] += _dot(act, wd_ref[...])

    @pl.when(f == pl.num_programs(1) - 1)
    def _():
        y_ref[...] = acc_scr[...]


def _ffn(n2, h, w_gate_up, w_down, *, bm, bf):
    m, d = h.shape
    d_ff = w_down.shape[0]
    nf = d_ff // bf
    row = pl.BlockSpec((bm, d), lambda i, f: (i, 0))
    return pl.pallas_call(
        _ffn_kernel,
        grid=(m // bm, nf),
        in_specs=[row, row,
                  pl.BlockSpec((d, bf), lambda i, f: (0, f)),
                  pl.BlockSpec((d, bf), lambda i, f: (0, f + nf)),
                  pl.BlockSpec((bf, d), lambda i, f: (f, 0))],
        out_specs=row,
        out_shape=jax.ShapeDtypeStruct((m, d), F32),
        scratch_shapes=[pltpu.VMEM((bm, d), F32)],
        compiler_params=_params("arbitrary", "arbitrary"),
    )(n2, h, w_gate_up, w_gate_up, w_down)


def _rope_tables(pos):
    freqs = ROPE_THETA ** (-jnp.arange(HALF_ROPE, dtype=F32) / HALF_ROPE)
    ang = pos.astype(F32)[:, None] * freqs
    c, s = jnp.cos(ang), jnp.sin(ang)
    return jnp.concatenate([c, c, c, c], axis=-1), jnp.concatenate([-s, s, -s, s], axis=-1)


def _swap_halves(a):
    return jnp.concatenate([a[..., HALF_ROPE:], a[..., :HALF_ROPE]], axis=-1)


def kernel(x_prompt, x_sample, cache_kv_latent, cache_k_rope, cache_k_inv_rms, state_conv, page_table,
           g_attn, w_in, b_gate, g_q_lat, g_kv_lat, w_uq, g_q_nope, g_q_rope, g_k_nope, g_k_rope,
           w_uk, w_uv, w_conv, w_o, g_ffn, w_gate_up, w_down):
    batch, seq, d = x_prompt.shape
    nseq, t_dec, _ = x_sample.shape
    past_len = page_table.shape[1] * PAGE_SIZE
    width = w_conv.shape[1]
    d_ff = w_down.shape[0]

    i1 = Q_RANK + KV_RANK
    i2 = i1 + ROPE_DIM
    i3, i4, i5 = i2 + width, i2 + 2 * width, i2 + 3 * width
    w_lat = jnp.concatenate([w_in[:, :i2], _swap_halves(w_in[:, i1:i2])], axis=1).astype(BF16)
    w_b, w_c, w_x = (w_in[:, a:a + width].astype(BF16) for a in (i2, i3, i4))
    w_ga, w_gc = w_in[:, i5:i5 + d].astype(BF16), w_in[:, i5 + d:].astype(BF16)
    b_ga, b_gc = b_gate[:d].reshape(1, d), b_gate[d:].reshape(1, d)
    uq_rope = w_uq[:, :, NOPE_DIM:]
    w_uq_all = jnp.concatenate([w_uq[:, :, :NOPE_DIM].reshape(Q_RANK, -1), uq_rope.reshape(Q_RANK, -1),
                                _swap_halves(uq_rope).reshape(Q_RANK, -1)], axis=1).astype(BF16)
    w_uk2 = w_uk.reshape(KV_RANK, -1).astype(BF16)
    w_uk_t = jnp.transpose(w_uk, (1, 2, 0)).astype(BF16)
    w_uv_t = jnp.transpose(w_uv, (1, 2, 0)).reshape(N_HEADS * V_DIM, KV_RANK).astype(BF16)
    w_uv_h = jnp.transpose(w_uv, (1, 0, 2)).astype(BF16)
    w_o_b, w_gu_b, w_down_b = w_o.astype(BF16), w_gate_up.astype(BF16), w_down.astype(BF16)
    row = lambda a: a.reshape(1, -1)
    g_kr = row(jnp.concatenate([g_k_rope, _swap_halves(g_k_rope)]))
    g_qr_a = row(jnp.concatenate([g_q_rope, g_q_rope]))
    g_qr_b = row(jnp.concatenate([_swap_halves(g_q_rope)] * 2))
    gains = [row(g_attn), row(g_q_lat), row(g_kv_lat), g_kr, row(g_q_nope), g_qr_a, g_qr_b, row(g_k_nope)]

    bm_lat = 256
    cos_p, sin_p = _rope_tables(jnp.arange(seq))
    cos_s, sin_s = _rope_tables(jnp.tile(past_len + jnp.arange(t_dec), bm_lat // t_dec))

    xp2 = x_prompt.reshape(batch * seq, d)
    b_p, u_p, ga_p, gc_p = _rest_proj(xp2, row(g_attn), w_b, w_c, w_x, w_ga, w_gc, b_ga, b_gc, bm=1024, bn=256)
    qcat, kcat, vt, ckv_p, krope_p, kinv_p = _latent_proj(
        True, x_prompt, cos_p, sin_p, gains, w_lat, w_uq_all, w_uk2, w_uv_t, bm=bm_lat)
    attn_p = _prompt_attn(qcat, kcat, vt, tq=512)
    to3 = lambda a: a.reshape(batch, seq, -1)
    u_p3 = to3(u_p)
    h_p, n2_p = _merge(False, x_prompt, attn_p, to3(b_p), u_p3, u_p3, to3(ga_p), to3(gc_p), w_conv, w_o_b,
                       row(g_ffn), bm=256)
    y_prompt = _ffn(n2_p.reshape(-1, d), h_p.reshape(-1, d), w_gu_b, w_down_b, bm=512, bf=512).reshape(batch, seq, d)
    conv_state_prompt = u_p3[:, seq - (CONV_K - 1):, :]

    m_s = nseq * t_dec
    xs2 = x_sample.reshape(m_s, d)
    b_s, u_s, ga_s, gc_s = _rest_proj(xs2, row(g_attn), w_b, w_c, w_x, w_ga, w_gc, b_ga, b_gc, bm=1024, bn=256)
    qa, qr, ckv_s, krope_s, kinv_s = _latent_proj(
        False, xs2.reshape(1, m_s, d), cos_s, sin_s, gains, w_lat, w_uq_all, w_uk2, w_uk_t, bm=bm_lat)
    ckv_s = ckv_s.reshape(nseq, t_dec, KV_RANK)
    krope_s = krope_s.reshape(nseq, t_dec, ROPE_DIM)
    kinv_s = kinv_s.reshape(nseq, t_dec, N_HEADS)
    g_new = jnp.repeat(jnp.swapaxes(kinv_s, 1, 2), t_dec, axis=1)
    g_new = jnp.concatenate([g_new, jnp.ones((nseq, N_HEADS * t_dec, PAGE_SIZE - t_dec), F32)], axis=-1)
    cache_inv_t = jnp.swapaxes(cache_k_inv_rms, 1, 2)
    o_lat = _decode_attn(page_table, qa, qr, ckv_s, krope_s, g_new, cache_kv_latent, cache_k_rope, cache_inv_t,
                         pages_per_step=16)
    attn_s = _uv_proj(o_lat.reshape(nseq, N_HEADS, t_dec, KV_RANK), w_uv_h)
    to1 = lambda a: a.reshape(1, m_s, -1)
    h_s, n2_s = _merge(True, to1(xs2), to1(attn_s), to1(b_s), to1(u_s), state_conv.astype(F32), to1(ga_s),
                       to1(gc_s), w_conv, w_o_b, row(g_ffn), bm=256)
    y_sample = _ffn(n2_s.reshape(-1, d), h_s.reshape(-1, d), w_gu_b, w_down_b, bm=512, bf=512).reshape(nseq, t_dec, d)
    u_s3 = u_s.reshape(nseq, t_dec, width)
    conv_state_sample = u_s3[:, t_dec - (CONV_K - 1):, :]

    return (y_prompt, y_sample, ckv_p, krope_p, kinv_p, conv_state_prompt,
            ckv_s, krope_s, kinv_s, conv_state_sample)
```

```python
import functools

import jax
import jax.numpy as jnp
from jax import lax
from jax.experimental import pallas as pl
from jax.experimental.pallas import tpu as pltpu

N_HEADS = 16
Q_RANK = 512
KV_RANK = 512
NOPE_DIM = 128
ROPE_DIM = 64
HALF_ROPE = ROPE_DIM // 2
V_DIM = 128
QK_DIM = NOPE_DIM + ROPE_DIM
QK_PAD = 256
ROPE_THETA = 10000.0
ATTN_SCALE = QK_DIM ** -0.5
PAGE_SIZE = 128
CONV_K = 3
EPS = 1e-6
NEG_INF = -1e30
LANES = 128
SUBLANES = 8
VMEM_LIMIT = 56 * 1024 * 1024

F32 = jnp.float32
BF16 = jnp.bfloat16
NT_DIMS = (((1,), (1,)), ((), ()))


def _params(*sem):
    return pltpu.CompilerParams(dimension_semantics=sem, vmem_limit_bytes=VMEM_LIMIT)


def _unit_rms(xf):
    return xf * lax.rsqrt(jnp.mean(xf * xf, axis=-1, keepdims=True) + EPS)


def _dot(a, b):
    return jnp.dot(a, b, preferred_element_type=F32)


def _dot_nt(a, b):
    return lax.dot_general(a, b, NT_DIMS, preferred_element_type=F32)


def _rest_kernel(x_ref, g_ref, wb_ref, wc_ref, wx_ref, wga_ref, wgc_ref, bga_ref, bgc_ref,
                 b_out, u_out, ga_out, gc_out, n_scr):
    @pl.when(pl.program_id(1) == 0)
    def _():
        n_scr[...] = (_unit_rms(x_ref[...]) * g_ref[...]).astype(BF16)

    n = n_scr[...]
    b_out[...] = _dot(n, wb_ref[...]).astype(BF16)
    u_out[...] = _dot(n, wc_ref[...]) * _dot(n, wx_ref[...])
    ga_out[...] = jax.nn.sigmoid(_dot(n, wga_ref[...]) + bga_ref[...]).astype(BF16)
    gc_out[...] = jax.nn.sigmoid(_dot(n, wgc_ref[...]) + bgc_ref[...]).astype(BF16)


def _rest_proj(x2, g_attn, w_b, w_c, w_x, w_ga, w_gc, b_ga, b_gc, *, bm, bn):
    m, d = x2.shape
    n = w_b.shape[1]
    row = lambda i, j: (i, 0)
    col = lambda i, j: (0, j)
    tile = lambda i, j: (i, j)
    w_spec = pl.BlockSpec((d, bn), col)
    v_spec = pl.BlockSpec((1, bn), col)
    o_spec = pl.BlockSpec((bm, bn), tile)
    return pl.pallas_call(
        _rest_kernel,
        grid=(m // bm, n // bn),
        in_specs=[pl.BlockSpec((bm, d), row), pl.BlockSpec((1, d), lambda i, j: (0, 0)),
                  w_spec, w_spec, w_spec, w_spec, w_spec, v_spec, v_spec],
        out_specs=[o_spec, o_spec, o_spec, o_spec],
        out_shape=[jax.ShapeDtypeStruct((m, n), BF16), jax.ShapeDtypeStruct((m, n), F32),
                   jax.ShapeDtypeStruct((m, n), BF16), jax.ShapeDtypeStruct((m, n), BF16)],
        scratch_shapes=[pltpu.VMEM((bm, d), BF16)],
        compiler_params=_params("arbitrary", "arbitrary"),
    )(x2, g_attn, w_b, w_c, w_x, w_ga, w_gc, b_ga, b_gc)


def _latent_kernel(prompt, x_ref, cos_ref, sin_ref, g_attn, g_q_lat, g_kv_lat, g_kr, g_q_nope, g_qr_a,
                   g_qr_b, g_k_nope, w_lat, w_uq, w_uk, w_aux, *outs):
    if prompt:
        qcat_out, kcat_out, vt_out, ckv_out, krope_out, kinv_out = outs
    else:
        qa_out, qr_out, ckv_out, krope_out, kinv_out = outs
    bm = x_ref.shape[1]
    cos_t = cos_ref[...]
    sin_t = sin_ref[...]
    lane = lax.broadcasted_iota(jnp.int32, (bm, LANES), 1)
    low_half = lane < ROPE_DIM

    n = (_unit_rms(x_ref[0]) * g_attn[...]).astype(BF16)
    lat = _dot(n, w_lat[...])
    cq = (_unit_rms(lat[:, :Q_RANK]) * g_q_lat[...]).astype(BF16)
    ckv = _unit_rms(lat[:, Q_RANK:Q_RANK + KV_RANK]) * g_kv_lat[...]
    ckv_out[0] = ckv
    ckv_b = ckv.astype(BF16)

    krp = lat[:, Q_RANK + KV_RANK:]
    kr_ms = jnp.sum(jnp.where(low_half, krp * krp, 0.0), axis=-1, keepdims=True) * (1.0 / ROPE_DIM)
    krp = krp * lax.rsqrt(kr_ms + EPS) * g_kr[...]
    krt = krp * jnp.where(low_half, cos_t, sin_t)
    kr2 = krt + pltpu.roll(krt, ROPE_DIM, axis=1)
    krope_out[0] = kr2[:, :ROPE_DIM]

    kraw = _dot(ckv_b, w_uk[...])
    kinv = jnp.zeros((bm, LANES), F32)
    for h in range(N_HEADS):
        kh = kraw[:, h * NOPE_DIM:(h + 1) * NOPE_DIM]
        inv_h = lax.rsqrt(jnp.mean(kh * kh, axis=-1, keepdims=True) + EPS)
        kinv = jnp.where(lane == h, inv_h, kinv)
        if prompt:
            kcat_out[0, h, :, :NOPE_DIM] = (kh * inv_h * g_k_nope[...]).astype(BF16)
            kcat_out[0, h, :, NOPE_DIM:] = kr2.astype(BF16)
    kinv_out[0] = kinv[:, :N_HEADS]

    if prompt:
        vt = _dot_nt(w_aux[...], ckv_b)
        vt_out[0, :, 0] = vt.reshape(N_HEADS, V_DIM, bm).astype(BF16)

    q = _dot(cq, w_uq[...])
    rope_a0 = N_HEADS * NOPE_DIM
    rope_b0 = rope_a0 + N_HEADS * ROPE_DIM
    for h in range(N_HEADS):
        qh = _unit_rms(q[:, h * NOPE_DIM:(h + 1) * NOPE_DIM]) * (g_q_nope[...] * ATTN_SCALE)
        if prompt:
            qcat_out[0, h, :, :NOPE_DIM] = qh.astype(BF16)
        else:
            qa_out[0, h] = _dot((qh * g_k_nope[...]).astype(BF16), w_aux[h])
    for p in range(N_HEADS // 2):
        qa_ = q[:, rope_a0 + p * LANES: rope_a0 + (p + 1) * LANES]
        qb_ = q[:, rope_b0 + p * LANES: rope_b0 + (p + 1) * LANES]
        sq = qa_ * qa_
        ms_lo = jnp.sum(jnp.where(low_half, sq, 0.0), axis=-1, keepdims=True) * (1.0 / ROPE_DIM)
        ms_hi = jnp.sum(jnp.where(low_half, 0.0, sq), axis=-1, keepdims=True) * (1.0 / ROPE_DIM)
        inv = jnp.where(low_half, lax.rsqrt(ms_lo + EPS), lax.rsqrt(ms_hi + EPS)) * ATTN_SCALE
        rot = (qa_ * g_qr_a[...] * cos_t + qb_ * g_qr_b[...] * sin_t) * inv
        even = jnp.where(low_half, rot, 0.0)
        odd = jnp.where(low_half, 0.0, rot)
        if prompt:
            qcat_out[0, 2 * p, :, NOPE_DIM:] = even.astype(BF16)
            qcat_out[0, 2 * p + 1, :, NOPE_DIM:] = odd.astype(BF16)
        else:
            qr_out[0, 2 * p] = even
            qr_out[0, 2 * p + 1] = odd


def _latent_proj(prompt, x3, cos_t, sin_t, gains, w_lat, w_uq, w_uk, w_aux, *, bm):
    g, r, d = x3.shape
    nblk = r // bm
    tab_blocks = cos_t.shape[0] // bm
    const2 = lambda a: pl.BlockSpec(a.shape, lambda b, i: (0, 0), pipeline_mode=pl.Buffered(1))
    const3 = lambda a: pl.BlockSpec(a.shape, lambda b, i: (0, 0, 0), pipeline_mode=pl.Buffered(1))
    tab_spec = pl.BlockSpec((bm, LANES), lambda b, i: (i % tab_blocks, 0))
    row3 = lambda w: pl.BlockSpec((1, bm, w), lambda b, i: (b, i, 0))
    head4 = lambda w: pl.BlockSpec((1, N_HEADS, bm, w), lambda b, i: (b, 0, i, 0))
    if prompt:
        out_specs = [head4(QK_PAD), head4(QK_PAD),
                     pl.BlockSpec((1, N_HEADS, 1, V_DIM, bm), lambda b, i: (b, 0, i, 0, 0)),
                     row3(KV_RANK), row3(ROPE_DIM), row3(N_HEADS)]
        out_shape = [jax.ShapeDtypeStruct((g, N_HEADS, r, QK_PAD), BF16),
                     jax.ShapeDtypeStruct((g, N_HEADS, r, QK_PAD), BF16),
                     jax.ShapeDtypeStruct((g, N_HEADS, nblk, V_DIM, bm), BF16)]
    else:
        out_specs = [head4(KV_RANK), head4(LANES), row3(KV_RANK), row3(ROPE_DIM), row3(N_HEADS)]
        out_shape = [jax.ShapeDtypeStruct((g, N_HEADS, r, KV_RANK), F32),
                     jax.ShapeDtypeStruct((g, N_HEADS, r, LANES), F32)]
    out_shape += [jax.ShapeDtypeStruct((g, r, KV_RANK), F32), jax.ShapeDtypeStruct((g, r, ROPE_DIM), F32),
                  jax.ShapeDtypeStruct((g, r, N_HEADS), F32)]
    aux_spec = const2(w_aux) if prompt else const3(w_aux)
    return pl.pallas_call(
        functools.partial(_latent_kernel, prompt),
        grid=(g, nblk),
        in_specs=[pl.BlockSpec((1, bm, d), lambda b, i: (b, i, 0)), tab_spec, tab_spec]
                 + [const2(a) for a in gains] + [const2(w_lat), const2(w_uq), const2(w_uk), aux_spec],
        out_specs=out_specs,
        out_shape=out_shape,
        compiler_params=_params("arbitrary", "arbitrary"),
    )(x3, cos_t, sin_t, *gains, w_lat, w_uq, w_uk, w_aux)


def _prompt_attn_kernel(q_ref, k_ref, vt_ref, o_ref, m_scr, l_scr, acc_scr, *, tq, vb, hps):
    qi = pl.program_id(2)
    m_scr[...] = jnp.full(m_scr.shape, NEG_INF, F32)
    l_scr[...] = jnp.zeros(l_scr.shape, F32)
    acc_scr[...] = jnp.zeros(acc_scr.shape, F32)

    def step(j, masked):
        for hh in range(hps):
            k = k_ref[0, hh, pl.ds(pl.multiple_of(j * tq, tq), tq), :]
            st = _dot_nt(k, q_ref[0, hh])
            if masked:
                kpos = lax.broadcasted_iota(jnp.int32, (tq, tq), 0)
                qpos = lax.broadcasted_iota(jnp.int32, (tq, tq), 1)
                st = jnp.where(kpos <= qpos, st, NEG_INF)
            m_prev = m_scr[hh]
            m_new = jnp.maximum(m_prev, jnp.max(st, axis=0, keepdims=True))
            p = jnp.exp(st - m_new)
            alpha = jnp.exp(m_prev - m_new)
            l_scr[hh] = alpha * l_scr[hh] + jnp.sum(p, axis=0, keepdims=True)
            pb = p.astype(BF16)
            pv = _dot(vt_ref[0, hh, j * (tq // vb)], pb[:vb])
            for s in range(1, tq // vb):
                pv += _dot(vt_ref[0, hh, j * (tq // vb) + s], pb[s * vb:(s + 1) * vb])
            acc_scr[hh] = alpha * acc_scr[hh] + pv
            m_scr[hh] = m_new

    def body(j, carry):
        step(j, False)
        return carry

    lax.fori_loop(0, qi, body, 0)
    step(qi, True)
    for hh in range(hps):
        o = acc_scr[hh] * (1.0 / l_scr[hh])
        o_ref[0, :, hh * V_DIM:(hh + 1) * V_DIM] = o.T.astype(BF16)


def _prompt_attn(qcat, kcat, vt, *, tq, heads_per_step):
    b, h, s, w = qcat.shape
    nvb, vb = vt.shape[2], vt.shape[4]
    hps = heads_per_step
    return pl.pallas_call(
        functools.partial(_prompt_attn_kernel, tq=tq, vb=vb, hps=hps),
        grid=(b, h // hps, s // tq),
        in_specs=[pl.BlockSpec((1, hps, tq, w), lambda bi, hi, qi: (bi, hi, qi, 0)),
                  pl.BlockSpec((1, hps, s, w), lambda bi, hi, qi: (bi, hi, 0, 0)),
                  pl.BlockSpec((1, hps, nvb, V_DIM, vb), lambda bi, hi, qi: (bi, hi, 0, 0, 0))],
        out_specs=pl.BlockSpec((1, tq, hps * V_DIM), lambda bi, hi, qi: (bi, qi, hi)),
        out_shape=jax.ShapeDtypeStruct((b, s, h * V_DIM), BF16),
        scratch_shapes=[pltpu.VMEM((hps, 1, tq), F32), pltpu.VMEM((hps, 1, tq), F32),
                        pltpu.VMEM((hps, V_DIM, tq), F32)],
        compiler_params=_params("arbitrary", "arbitrary", "arbitrary"),
    )(qcat, kcat, vt)


def _expand_heads(inv_t):
    k = inv_t.shape[1]
    return jnp.broadcast_to(inv_t[:, None, :], (N_HEADS, SUBLANES, k)).reshape(N_HEADS * SUBLANES, k)


def _decode_kernel(nchunk, gpages, pt_ref, qa_ref, qr_ref, cnew_ref, krnew_ref, gnew_ref,
                   cache_c, cache_kr, cache_inv, o_ref,
                   raw_c0, raw_c1, raw_kr0, raw_kr1, raw_inv0, raw_inv1, cb0, cb1, s0, s1,
                   qa_scr, qr_scr, m_scr, l_scr, acc_scr, sems):
    b = pl.program_id(0)
    nseq = pl.num_programs(0)
    rows = N_HEADS * SUBLANES
    raws = ((raw_c0, raw_kr0, raw_inv0), (raw_c1, raw_kr1, raw_inv1))
    cbs, ss = (cb0, cb1), (s0, s1)

    def page_copies(seq, chunk, slot):
        raw_c, raw_kr, raw_inv = raws[slot]
        out = []
        for p in range(gpages):
            page = pt_ref[seq, chunk * gpages + p]
            out.append(pltpu.make_async_copy(cache_c.at[page], raw_c.at[p], sems.at[slot, 0]))
            out.append(pltpu.make_async_copy(cache_kr.at[page], raw_kr.at[p], sems.at[slot, 1]))
            out.append(pltpu.make_async_copy(cache_inv.at[page], raw_inv.at[p], sems.at[slot, 2]))
        return out

    def start_chunk(seq, chunk, slot):
        for cp in page_copies(seq, chunk, slot):
            cp.start()

    @pl.when(b == 0)
    def _():
        start_chunk(0, 0, 0)
        start_chunk(0, 1, 1)

    qa_scr[...] = qa_ref[0].reshape(rows, KV_RANK).astype(BF16)
    qr_in = qr_ref[0].reshape(rows, LANES)
    qr_scr[...] = (qr_in[:, :ROPE_DIM] + qr_in[:, ROPE_DIM:]).astype(BF16)
    m_scr[...] = jnp.full(m_scr.shape, NEG_INF, F32)
    l_scr[...] = jnp.zeros(l_scr.shape, F32)
    acc_scr[...] = jnp.zeros(acc_scr.shape, F32)

    def stage_a(chunk, slot):
        raw_c, raw_kr, raw_inv = raws[slot]
        for cp in page_copies(b, chunk, slot):
            cp.wait()
        qa = qa_scr[...]
        qr = qr_scr[...]
        for p in range(gpages):
            cb = raw_c[p].astype(BF16)
            cbs[slot][p * PAGE_SIZE:(p + 1) * PAGE_SIZE, :] = cb
            inv_t = raw_inv[p].T
            s_nope = _dot_nt(qa, cb) * _expand_heads(inv_t)
            s_rope = _dot_nt(qr, raw_kr[p].astype(BF16))
            ss[slot][:, p * PAGE_SIZE:(p + 1) * PAGE_SIZE] = s_nope + s_rope
        nxt = chunk + 2
        wraps = nxt >= nchunk
        seq2 = jnp.where(wraps, b + 1, b)
        chunk2 = jnp.where(wraps, nxt - nchunk, nxt)

        @pl.when(seq2 < nseq)
        def _():
            start_chunk(seq2, chunk2, slot)

    def update(s, values_b):
        m_prev = m_scr[...]
        m_new = jnp.maximum(m_prev, jnp.max(s, axis=-1, keepdims=True))
        p = jnp.exp(s - m_new)
        alpha = jnp.exp(m_prev - m_new)
        l_scr[...] = alpha * l_scr[...] + jnp.sum(p, axis=-1, keepdims=True)
        acc_scr[...] = alpha * acc_scr[...] + _dot(p.astype(BF16), values_b)
        m_scr[...] = m_new

    def stage_b(slot):
        update(ss[slot][...], cbs[slot][...])

    stage_a(0, 0)

    def pair(cc, carry):
        stage_a(2 * cc + 1, 1)
        stage_b(0)
        stage_a(2 * cc + 2, 0)
        stage_b(1)
        return carry

    lax.fori_loop(0, (nchunk - 2) // 2, pair, 0)
    stage_a(nchunk - 1, 1)
    stage_b(0)
    stage_b(1)

    t_new = cnew_ref.shape[1]
    pad = lambda a: jnp.concatenate([a, jnp.zeros((PAGE_SIZE - t_new, a.shape[1]), F32)], axis=0).astype(BF16)
    cn = pad(cnew_ref[0])
    s_new = _dot_nt(qa_scr[...], cn) * gnew_ref[0] + _dot_nt(qr_scr[...], pad(krnew_ref[0]))
    tok = lax.broadcasted_iota(jnp.int32, (rows, PAGE_SIZE), 0) % SUBLANES
    key = lax.broadcasted_iota(jnp.int32, (rows, PAGE_SIZE), 1)
    update(jnp.where(key <= tok, s_new, NEG_INF), cn)
    o_ref[0] = acc_scr[...] * (1.0 / l_scr[...])


def _decode_attn(page_table, qa, qr, c_new, kr_new, g_new, cache_c, cache_kr, cache_inv, *, pages_per_chunk):
    nseq, npages = page_table.shape
    t_new = c_new.shape[1]
    gpages = pages_per_chunk
    nchunk = npages // gpages
    assert t_new == SUBLANES and npages % gpages == 0 and nchunk % 2 == 0 and nchunk >= 4
    rows = N_HEADS * t_new
    keys = gpages * PAGE_SIZE
    any_spec = pl.BlockSpec(memory_space=pl.ANY)
    in_specs = [pl.BlockSpec((1, N_HEADS, t_new, KV_RANK), lambda b, pt: (0, 0, b, 0)),
                pl.BlockSpec((1, N_HEADS, t_new, LANES), lambda b, pt: (0, 0, b, 0)),
                pl.BlockSpec((1, t_new, KV_RANK), lambda b, pt: (b, 0, 0)),
                pl.BlockSpec((1, t_new, ROPE_DIM), lambda b, pt: (b, 0, 0)),
                pl.BlockSpec((1, rows, PAGE_SIZE), lambda b, pt: (b, 0, 0)),
                any_spec, any_spec, any_spec]
    two = lambda shape, dtype: [pltpu.VMEM(shape, dtype), pltpu.VMEM(shape, dtype)]
    grid_spec = pltpu.PrefetchScalarGridSpec(
        num_scalar_prefetch=1,
        grid=(nseq,),
        in_specs=in_specs,
        out_specs=pl.BlockSpec((1, rows, KV_RANK), lambda b, pt: (b, 0, 0)),
        scratch_shapes=two((gpages, PAGE_SIZE, KV_RANK), F32) + two((gpages, PAGE_SIZE, ROPE_DIM), F32)
                       + two((gpages, PAGE_SIZE, N_HEADS), F32) + two((keys, KV_RANK), BF16) + two((rows, keys), F32)
                       + [pltpu.VMEM((rows, KV_RANK), BF16), pltpu.VMEM((rows, ROPE_DIM), BF16),
                          pltpu.VMEM((rows, 1), F32), pltpu.VMEM((rows, 1), F32), pltpu.VMEM((rows, KV_RANK), F32),
                          pltpu.SemaphoreType.DMA((2, 3))],
    )
    return pl.pallas_call(
        functools.partial(_decode_kernel, nchunk, gpages),
        grid_spec=grid_spec,
        out_shape=jax.ShapeDtypeStruct((nseq, rows, KV_RANK), F32),
        compiler_params=_params("arbitrary"),
    )(page_table, qa, qr, c_new, kr_new, g_new, cache_c, cache_kr, cache_inv)


def _uv_kernel(o_ref, w_ref, out_ref):
    nseq, _, t, r = o_ref.shape
    o = o_ref[...].reshape(nseq * t, r).astype(BF16)
    out_ref[...] = _dot(o, w_ref[0]).astype(BF16)


def _uv_proj(o_lat4, w_uv_h):
    nseq, h, t, r = o_lat4.shape
    return pl.pallas_call(
        _uv_kernel,
        grid=(h,),
        in_specs=[pl.BlockSpec((nseq, 1, t, r), lambda i: (0, i, 0, 0)),
                  pl.BlockSpec((1, r, V_DIM), lambda i: (i, 0, 0))],
        out_specs=pl.BlockSpec((nseq * t, V_DIM), lambda i: (0, i)),
        out_shape=jax.ShapeDtypeStruct((nseq * t, h * V_DIM), BF16),
        compiler_params=_params("arbitrary"),
    )(o_lat4, w_uv_h)


def _merge_kernel(short_seq, x_ref, attn_ref, b_ref, u_ref, prev_ref, ga_ref, gc_ref, wconv_ref, wo_ref,
                  gffn_ref, h_out, n2_out):
    u = u_ref[...]
    bm, c = u.shape
    if short_seq:
        nseq = bm // SUBLANES
        u3 = u.reshape(nseq, SUBLANES, c)
        tok = lax.broadcasted_iota(jnp.int32, (nseq, SUBLANES, c), 1)
        st = prev_ref[...]
        s0, s1 = st[:, 0:1, :], st[:, 1:2, :]
        u1 = jnp.where(tok == 0, s1, pltpu.roll(u3, 1, axis=1))
        u2 = jnp.where(tok == 0, s0, jnp.where(tok == 1, s1, pltpu.roll(u3, 2, axis=1)))
        u1 = u1.reshape(bm, c)
        u2 = u2.reshape(bm, c)
    else:
        first = pl.program_id(1) == 0
        halo = jnp.where(first, 0.0, prev_ref[0])
        row = lax.broadcasted_iota(jnp.int32, (bm, c), 0)
        h1, h2 = halo[SUBLANES - 1:SUBLANES, :], halo[SUBLANES - 2:SUBLANES - 1, :]
        u1 = jnp.where(row == 0, h1, pltpu.roll(u, 1, axis=0))
        u2 = jnp.where(row == 0, h2, jnp.where(row == 1, h1, pltpu.roll(u, 2, axis=0)))
    wc = wconv_ref[...]
    conv = wc[0:1, :] * u2 + wc[1:2, :] * u1 + wc[2:3, :] * u
    mixed = ga_ref[...].astype(F32) * attn_ref[...].astype(F32) + gc_ref[...].astype(F32) * (b_ref[...].astype(F32) * conv)
    h = x_ref[...] + _dot(mixed.astype(BF16), wo_ref[...])
    h_out[...] = h
    n2_out[...] = (_unit_rms(h) * gffn_ref[...]).astype(BF16)


def _merge(short_seq, x3, attn3, b3, u3, prev, ga3, gc3, w_conv, w_o, g_ffn, *, bm):
    g, r, d = x3.shape
    row = pl.BlockSpec((None, bm, d), lambda b, i: (b, i, 0))
    if short_seq:
        prev_spec = pl.BlockSpec((bm // SUBLANES, CONV_K - 1, d), lambda b, i: (i, 0, 0))
    else:
        per = bm // SUBLANES
        prev_spec = pl.BlockSpec((1, SUBLANES, d), lambda b, i: (b, jnp.maximum(i * per - 1, 0), 0))
    return pl.pallas_call(
        functools.partial(_merge_kernel, short_seq),
        grid=(g, r // bm),
        in_specs=[row, row, row, row, prev_spec, row, row,
                  pl.BlockSpec(w_conv.shape, lambda b, i: (0, 0)),
                  pl.BlockSpec(w_o.shape, lambda b, i: (0, 0), pipeline_mode=pl.Buffered(1)),
                  pl.BlockSpec(g_ffn.shape, lambda b, i: (0, 0))],
        out_specs=[row, row],
        out_shape=[jax.ShapeDtypeStruct((g, r, d), F32), jax.ShapeDtypeStruct((g, r, d), BF16)],
        compiler_params=_params("arbitrary", "arbitrary"),
    )(x3, attn3, b3, u3, prev, ga3, gc3, w_conv, w_o, g_ffn)


def _ffn_kernel(n2_ref, h_ref, wg_ref, wu_ref, wd_ref, y_ref, acc_scr):
    f = pl.program_id(1)

    @pl.when(f == 0)
    def _():
        acc_scr[...] = h_ref[...]

    n2 = n2_ref[...]
    act = (jax.nn.silu(_dot(n2, wg_ref[...])) * _dot(n2, wu_ref[...])).astype(BF16)
    acc_scr[...] += _dot(act, wd_ref[...])

    @pl.when(f == pl.num_programs(1) - 1)
    def _():
        y_ref[...] = acc_scr[...]


def _ffn(n2, h, w_gate_up, w_down, *, bm, bf):
    m, d = h.shape
    d_ff = w_down.shape[0]
    nf = d_ff // bf
    row = pl.BlockSpec((bm, d), lambda i, f: (i, 0))
    return pl.pallas_call(
        _ffn_kernel,
        grid=(m // bm, nf),
        in_specs=[row, row,
                  pl.BlockSpec((d, bf), lambda i, f: (0, f)),
                  pl.BlockSpec((d, bf), lambda i, f: (0, f + nf)),
                  pl.BlockSpec((bf, d), lambda i, f: (f, 0))],
        out_specs=row,
        out_shape=jax.ShapeDtypeStruct((m, d), F32),
        scratch_shapes=[pltpu.VMEM((bm, d), F32)],
        compiler_params=_params("arbitrary", "arbitrary"),
    )(n2, h, w_gate_up, w_gate_up, w_down)


def _rope_tables(pos):
    freqs = ROPE_THETA ** (-jnp.arange(HALF_ROPE, dtype=F32) / HALF_ROPE)
    ang = pos.astype(F32)[:, None] * freqs
    c, s = jnp.cos(ang), jnp.sin(ang)
    return jnp.concatenate([c, c, c, c], axis=-1), jnp.concatenate([-s, s, -s, s], axis=-1)


def _swap_halves(a):
    return jnp.concatenate([a[..., HALF_ROPE:], a[..., :HALF_ROPE]], axis=-1)


def kernel(x_prompt, x_sample, cache_kv_latent, cache_k_rope, cache_k_inv_rms, state_conv, page_table,
           g_attn, w_in, b_gate, g_q_lat, g_kv_lat, w_uq, g_q_nope, g_q_rope, g_k_nope, g_k_rope,
           w_uk, w_uv, w_conv, w_o, g_ffn, w_gate_up, w_down):
    batch, seq, d = x_prompt.shape
    nseq, t_dec, _ = x_sample.shape
    past_len = page_table.shape[1] * PAGE_SIZE
    width = w_conv.shape[1]
    d_ff = w_down.shape[0]

    i1 = Q_RANK + KV_RANK
    i2 = i1 + ROPE_DIM
    i3, i4, i5 = i2 + width, i2 + 2 * width, i2 + 3 * width
    w_lat = jnp.concatenate([w_in[:, :i2], _swap_halves(w_in[:, i1:i2])], axis=1).astype(BF16)
    w_b, w_c, w_x = (w_in[:, a:a + width].astype(BF16) for a in (i2, i3, i4))
    w_ga, w_gc = w_in[:, i5:i5 + d].astype(BF16), w_in[:, i5 + d:].astype(BF16)
    b_ga, b_gc = b_gate[:d].reshape(1, d), b_gate[d:].reshape(1, d)
    uq_rope = w_uq[:, :, NOPE_DIM:]
    w_uq_all = jnp.concatenate([w_uq[:, :, :NOPE_DIM].reshape(Q_RANK, -1), uq_rope.reshape(Q_RANK, -1),
                                _swap_halves(uq_rope).reshape(Q_RANK, -1)], axis=1).astype(BF16)
    w_uk2 = w_uk.reshape(KV_RANK, -1).astype(BF16)
    w_uk_t = jnp.transpose(w_uk, (1, 2, 0)).astype(BF16)
    w_uv_t = jnp.transpose(w_uv, (1, 2, 0)).reshape(N_HEADS * V_DIM, KV_RANK).astype(BF16)
    w_uv_h = jnp.transpose(w_uv, (1, 0, 2)).astype(BF16)
    w_o_b, w_gu_b, w_down_b = w_o.astype(BF16), w_gate_up.astype(BF16), w_down.astype(BF16)
    row = lambda a: a.reshape(1, -1)
    g_kr = row(jnp.concatenate([g_k_rope, _swap_halves(g_k_rope)]))
    g_qr_a = row(jnp.concatenate([g_q_rope, g_q_rope]))
    g_qr_b = row(jnp.concatenate([_swap_halves(g_q_rope)] * 2))
    gains = [row(g_attn), row(g_q_lat), row(g_kv_lat), g_kr, row(g_q_nope), g_qr_a, g_qr_b, row(g_k_nope)]

    bm_lat = 256
    cos_p, sin_p = _rope_tables(jnp.arange(seq))
    cos_s, sin_s = _rope_tables(jnp.tile(past_len + jnp.arange(t_dec), bm_lat // t_dec))

    xp2 = x_prompt.reshape(batch * seq, d)
    b_p, u_p, ga_p, gc_p = _rest_proj(xp2, row(g_attn), w_b, w_c, w_x, w_ga, w_gc, b_ga, b_gc, bm=1024, bn=256)
    qcat, kcat, vt, ckv_p, krope_p, kinv_p = _latent_proj(
        True, x_prompt, cos_p, sin_p, gains, w_lat, w_uq_all, w_uk2, w_uv_t, bm=bm_lat)
    attn_p = _prompt_attn(qcat, kcat, vt, tq=512, heads_per_step=2)
    to3 = lambda a: a.reshape(batch, seq, -1)
    u_p3 = to3(u_p)
    h_p, n2_p = _merge(False, x_prompt, attn_p, to3(b_p), u_p3, u_p3, to3(ga_p), to3(gc_p), w_conv, w_o_b,
                       row(g_ffn), bm=256)
    y_prompt = _ffn(n2_p.reshape(-1, d), h_p.reshape(-1, d), w_gu_b, w_down_b, bm=512, bf=512).reshape(batch, seq, d)
    conv_state_prompt = u_p3[:, seq - (CONV_K - 1):, :]

    m_s = nseq * t_dec
    xs2 = x_sample.reshape(m_s, d)
    b_s, u_s, ga_s, gc_s = _rest_proj(xs2, row(g_attn), w_b, w_c, w_x, w_ga, w_gc, b_ga, b_gc, bm=1024, bn=256)
    qa, qr, ckv_s, krope_s, kinv_s = _latent_proj(
        False, xs2.reshape(1, m_s, d), cos_s, sin_s, gains, w_lat, w_uq_all, w_uk2, w_uk_t, bm=bm_lat)
    ckv_s = ckv_s.reshape(nseq, t_dec, KV_RANK)
    krope_s = krope_s.reshape(nseq, t_dec, ROPE_DIM)
    kinv_s = kinv_s.reshape(nseq, t_dec, N_HEADS)
    g_new = jnp.repeat(jnp.swapaxes(kinv_s, 1, 2), t_dec, axis=1)
    g_new = jnp.concatenate([g_new, jnp.ones((nseq, N_HEADS * t_dec, PAGE_SIZE - t_dec), F32)], axis=-1)
    o_lat = _decode_attn(page_table, qa, qr, ckv_s, krope_s, g_new, cache_kv_latent, cache_k_rope,
                         cache_k_inv_rms, pages_per_chunk=16)
    attn_s = _uv_proj(o_lat.reshape(nseq, N_HEADS, t_dec, KV_RANK), w_uv_h)
    to1 = lambda a: a.reshape(1, m_s, -1)
    h_s, n2_s = _merge(True, to1(xs2), to1(attn_s), to1(b_s), to1(u_s), state_conv.astype(F32), to1(ga_s),
                       to1(gc_s), w_conv, w_o_b, row(g_ffn), bm=256)
    y_sample = _ffn(n2_s.reshape(-1, d), h_s.reshape(-1, d), w_gu_b, w_down_b, bm=512, bf=512).reshape(nseq, t_dec, d)
    u_s3 = u_s.reshape(nseq, t_dec, width)
    conv_state_sample = u_s3[:, t_dec - (CONV_K - 1):, :]

    return (y_prompt, y_sample, ckv_p, krope_p, kinv_p, conv_state_prompt,
            ckv_s, krope_s, kinv_s, conv_state_sample)
```

```python
import functools

import jax
import jax.numpy as jnp
from jax import lax
from jax.experimental import pallas as pl
from jax.experimental.pallas import tpu as pltpu

N_HEADS = 16
Q_RANK = 512
KV_RANK = 512
NOPE_DIM = 128
ROPE_DIM = 64
HALF_ROPE = ROPE_DIM // 2
V_DIM = 128
QK_DIM = NOPE_DIM + ROPE_DIM
QK_PAD = 256
ROPE_THETA = 10000.0
ATTN_SCALE = QK_DIM ** -0.5
PAGE_SIZE = 128
CONV_K = 3
EPS = 1e-6
NEG_INF = -1e30
LANES = 128
SUBLANES = 8
VMEM_LIMIT = 56 * 1024 * 1024

F32 = jnp.float32
BF16 = jnp.bfloat16
NT_DIMS = (((1,), (1,)), ((), ()))


def _params(*sem):
    return pltpu.CompilerParams(dimension_semantics=sem, vmem_limit_bytes=VMEM_LIMIT)


def _unit_rms(xf):
    return xf * lax.rsqrt(jnp.mean(xf * xf, axis=-1, keepdims=True) + EPS)


def _dot(a, b):
    return jnp.dot(a, b, preferred_element_type=F32)


def _dot_nt(a, b):
    return lax.dot_general(a, b, NT_DIMS, preferred_element_type=F32)


def _rest_kernel(x_ref, g_ref, wb_ref, wc_ref, wx_ref, wga_ref, wgc_ref, bga_ref, bgc_ref,
                 b_out, u_out, ga_out, gc_out, n_scr):
    @pl.when(pl.program_id(1) == 0)
    def _():
        n_scr[...] = (_unit_rms(x_ref[...]) * g_ref[...]).astype(BF16)

    n = n_scr[...]
    b_out[...] = _dot(n, wb_ref[...]).astype(BF16)
    u_out[...] = _dot(n, wc_ref[...]) * _dot(n, wx_ref[...])
    ga_out[...] = jax.nn.sigmoid(_dot(n, wga_ref[...]) + bga_ref[...]).astype(BF16)
    gc_out[...] = jax.nn.sigmoid(_dot(n, wgc_ref[...]) + bgc_ref[...]).astype(BF16)


def _rest_proj(x2, g_attn, w_b, w_c, w_x, w_ga, w_gc, b_ga, b_gc, *, bm, bn):
    m, d = x2.shape
    n = w_b.shape[1]
    row = lambda i, j: (i, 0)
    col = lambda i, j: (0, j)
    tile = lambda i, j: (i, j)
    w_spec = pl.BlockSpec((d, bn), col)
    v_spec = pl.BlockSpec((1, bn), col)
    o_spec = pl.BlockSpec((bm, bn), tile)
    return pl.pallas_call(
        _rest_kernel,
        grid=(m // bm, n // bn),
        in_specs=[pl.BlockSpec((bm, d), row), pl.BlockSpec((1, d), lambda i, j: (0, 0)),
                  w_spec, w_spec, w_spec, w_spec, w_spec, v_spec, v_spec],
        out_specs=[o_spec, o_spec, o_spec, o_spec],
        out_shape=[jax.ShapeDtypeStruct((m, n), BF16), jax.ShapeDtypeStruct((m, n), F32),
                   jax.ShapeDtypeStruct((m, n), BF16), jax.ShapeDtypeStruct((m, n), BF16)],
        scratch_shapes=[pltpu.VMEM((bm, d), BF16)],
        compiler_params=_params("arbitrary", "arbitrary"),
    )(x2, g_attn, w_b, w_c, w_x, w_ga, w_gc, b_ga, b_gc)


def _latent_kernel(prompt, x_ref, cos_ref, sin_ref, g_attn, g_q_lat, g_kv_lat, g_kr, g_q_nope, g_qr_a,
                   g_qr_b, g_k_nope, w_lat, w_uq, w_uk, w_aux, *outs):
    if prompt:
        qcat_out, kcat_out, vt_out, ckv_out, krope_out, kinv_out = outs
    else:
        qa_out, qr_out, ckv_out, krope_out, kinv_out = outs
    bm = x_ref.shape[1]
    cos_t = cos_ref[...]
    sin_t = sin_ref[...]
    lane = lax.broadcasted_iota(jnp.int32, (bm, LANES), 1)
    low_half = lane < ROPE_DIM

    n = (_unit_rms(x_ref[0]) * g_attn[...]).astype(BF16)
    lat = _dot(n, w_lat[...])
    cq = (_unit_rms(lat[:, :Q_RANK]) * g_q_lat[...]).astype(BF16)
    ckv = _unit_rms(lat[:, Q_RANK:Q_RANK + KV_RANK]) * g_kv_lat[...]
    ckv_out[0] = ckv
    ckv_b = ckv.astype(BF16)

    krp = lat[:, Q_RANK + KV_RANK:]
    kr_ms = jnp.sum(jnp.where(low_half, krp * krp, 0.0), axis=-1, keepdims=True) * (1.0 / ROPE_DIM)
    krp = krp * lax.rsqrt(kr_ms + EPS) * g_kr[...]
    krt = krp * jnp.where(low_half, cos_t, sin_t)
    kr2 = krt + pltpu.roll(krt, ROPE_DIM, axis=1)
    krope_out[0] = kr2[:, :ROPE_DIM]

    kraw = _dot(ckv_b, w_uk[...])
    kinv = jnp.zeros((bm, LANES), F32)
    for h in range(N_HEADS):
        kh = kraw[:, h * NOPE_DIM:(h + 1) * NOPE_DIM]
        inv_h = lax.rsqrt(jnp.mean(kh * kh, axis=-1, keepdims=True) + EPS)
        kinv = jnp.where(lane == h, inv_h, kinv)
        if prompt:
            kcat_out[0, h, :, :NOPE_DIM] = (kh * inv_h * g_k_nope[...]).astype(BF16)
            kcat_out[0, h, :, NOPE_DIM:] = kr2.astype(BF16)
    kinv_out[0] = kinv[:, :N_HEADS]

    if prompt:
        vt = _dot_nt(w_aux[...], ckv_b)
        vt_out[0, :, 0] = vt.reshape(N_HEADS, V_DIM, bm).astype(BF16)

    q = _dot(cq, w_uq[...])
    rope_a0 = N_HEADS * NOPE_DIM
    rope_b0 = rope_a0 + N_HEADS * ROPE_DIM
    for h in range(N_HEADS):
        qh = _unit_rms(q[:, h * NOPE_DIM:(h + 1) * NOPE_DIM]) * (g_q_nope[...] * ATTN_SCALE)
        if prompt:
            qcat_out[0, h, :, :NOPE_DIM] = qh.astype(BF16)
        else:
            qa_out[0, h] = _dot((qh * g_k_nope[...]).astype(BF16), w_aux[h])
    for p in range(N_HEADS // 2):
        qa_ = q[:, rope_a0 + p * LANES: rope_a0 + (p + 1) * LANES]
        qb_ = q[:, rope_b0 + p * LANES: rope_b0 + (p + 1) * LANES]
        sq = qa_ * qa_
        ms_lo = jnp.sum(jnp.where(low_half, sq, 0.0), axis=-1, keepdims=True) * (1.0 / ROPE_DIM)
        ms_hi = jnp.sum(jnp.where(low_half, 0.0, sq), axis=-1, keepdims=True) * (1.0 / ROPE_DIM)
        inv = jnp.where(low_half, lax.rsqrt(ms_lo + EPS), lax.rsqrt(ms_hi + EPS)) * ATTN_SCALE
        rot = (qa_ * g_qr_a[...] * cos_t + qb_ * g_qr_b[...] * sin_t) * inv
        even = jnp.where(low_half, rot, 0.0)
        odd = jnp.where(low_half, 0.0, rot)
        if prompt:
            qcat_out[0, 2 * p, :, NOPE_DIM:] = even.astype(BF16)
            qcat_out[0, 2 * p + 1, :, NOPE_DIM:] = odd.astype(BF16)
        else:
            qr_out[0, 2 * p] = even
            qr_out[0, 2 * p + 1] = odd


def _latent_proj(prompt, x3, cos_t, sin_t, gains, w_lat, w_uq, w_uk, w_aux, *, bm):
    g, r, d = x3.shape
    nblk = r // bm
    tab_blocks = cos_t.shape[0] // bm
    const2 = lambda a: pl.BlockSpec(a.shape, lambda b, i: (0, 0), pipeline_mode=pl.Buffered(1))
    const3 = lambda a: pl.BlockSpec(a.shape, lambda b, i: (0, 0, 0), pipeline_mode=pl.Buffered(1))
    tab_spec = pl.BlockSpec((bm, LANES), lambda b, i: (i % tab_blocks, 0))
    row3 = lambda w: pl.BlockSpec((1, bm, w), lambda b, i: (b, i, 0))
    head4 = lambda w: pl.BlockSpec((1, N_HEADS, bm, w), lambda b, i: (b, 0, i, 0))
    if prompt:
        out_specs = [head4(QK_PAD), head4(QK_PAD),
                     pl.BlockSpec((1, N_HEADS, 1, V_DIM, bm), lambda b, i: (b, 0, i, 0, 0)),
                     row3(KV_RANK), row3(ROPE_DIM), row3(N_HEADS)]
        out_shape = [jax.ShapeDtypeStruct((g, N_HEADS, r, QK_PAD), BF16),
                     jax.ShapeDtypeStruct((g, N_HEADS, r, QK_PAD), BF16),
                     jax.ShapeDtypeStruct((g, N_HEADS, nblk, V_DIM, bm), BF16)]
    else:
        out_specs = [head4(KV_RANK), head4(LANES), row3(KV_RANK), row3(ROPE_DIM), row3(N_HEADS)]
        out_shape = [jax.ShapeDtypeStruct((g, N_HEADS, r, KV_RANK), F32),
                     jax.ShapeDtypeStruct((g, N_HEADS, r, LANES), F32)]
    out_shape += [jax.ShapeDtypeStruct((g, r, KV_RANK), F32), jax.ShapeDtypeStruct((g, r, ROPE_DIM), F32),
                  jax.ShapeDtypeStruct((g, r, N_HEADS), F32)]
    aux_spec = const2(w_aux) if prompt else const3(w_aux)
    return pl.pallas_call(
        functools.partial(_latent_kernel, prompt),
        grid=(g, nblk),
        in_specs=[pl.BlockSpec((1, bm, d), lambda b, i: (b, i, 0)), tab_spec, tab_spec]
                 + [const2(a) for a in gains] + [const2(w_lat), const2(w_uq), const2(w_uk), aux_spec],
        out_specs=out_specs,
        out_shape=out_shape,
        compiler_params=_params("arbitrary", "arbitrary"),
    )(x3, cos_t, sin_t, *gains, w_lat, w_uq, w_uk, w_aux)


def _prompt_attn_kernel(q_ref, k_ref, vt_ref, o_ref, m_scr, l_scr, acc_scr, *, tq, vb, hps):
    qi = pl.program_id(2)
    m_scr[...] = jnp.full(m_scr.shape, NEG_INF, F32)
    l_scr[...] = jnp.zeros(l_scr.shape, F32)
    acc_scr[...] = jnp.zeros(acc_scr.shape, F32)

    def step(j, masked):
        for hh in range(hps):
            k = k_ref[0, hh, pl.ds(pl.multiple_of(j * tq, tq), tq), :]
            st = _dot_nt(k, q_ref[0, hh])
            if masked:
                kpos = lax.broadcasted_iota(jnp.int32, (tq, tq), 0)
                qpos = lax.broadcasted_iota(jnp.int32, (tq, tq), 1)
                st = jnp.where(kpos <= qpos, st, NEG_INF)
            m_prev = m_scr[hh]
            m_new = jnp.maximum(m_prev, jnp.max(st, axis=0, keepdims=True))
            p = jnp.exp(st - m_new)
            alpha = jnp.exp(m_prev - m_new)
            l_scr[hh] = alpha * l_scr[hh] + jnp.sum(p, axis=0, keepdims=True)
            pb = p.astype(BF16)
            pv = _dot(vt_ref[0, hh, j * (tq // vb)], pb[:vb])
            for s in range(1, tq // vb):
                pv += _dot(vt_ref[0, hh, j * (tq // vb) + s], pb[s * vb:(s + 1) * vb])
            acc_scr[hh] = alpha * acc_scr[hh] + pv
            m_scr[hh] = m_new

    def body(j, carry):
        step(j, False)
        return carry

    lax.fori_loop(0, qi, body, 0)
    step(qi, True)
    for hh in range(hps):
        o = acc_scr[hh] * (1.0 / l_scr[hh])
        o_ref[0, :, hh * V_DIM:(hh + 1) * V_DIM] = o.T.astype(BF16)


def _prompt_attn(qcat, kcat, vt, *, tq, heads_per_step):
    b, h, s, w = qcat.shape
    nvb, vb = vt.shape[2], vt.shape[4]
    hps = heads_per_step
    return pl.pallas_call(
        functools.partial(_prompt_attn_kernel, tq=tq, vb=vb, hps=hps),
        grid=(b, h // hps, s // tq),
        in_specs=[pl.BlockSpec((1, hps, tq, w), lambda bi, hi, qi: (bi, hi, qi, 0)),
                  pl.BlockSpec((1, hps, s, w), lambda bi, hi, qi: (bi, hi, 0, 0)),
                  pl.BlockSpec((1, hps, nvb, V_DIM, vb), lambda bi, hi, qi: (bi, hi, 0, 0, 0))],
        out_specs=pl.BlockSpec((1, tq, hps * V_DIM), lambda bi, hi, qi: (bi, qi, hi)),
        out_shape=jax.ShapeDtypeStruct((b, s, h * V_DIM), BF16),
        scratch_shapes=[pltpu.VMEM((hps, 1, tq), F32), pltpu.VMEM((hps, 1, tq), F32),
                        pltpu.VMEM((hps, V_DIM, tq), F32)],
        compiler_params=_params("arbitrary", "arbitrary", "arbitrary"),
    )(qcat, kcat, vt)


def _expand_heads(inv_t):
    k = inv_t.shape[1]
    return jnp.broadcast_to(inv_t[:, None, :], (N_HEADS, SUBLANES, k)).reshape(N_HEADS * SUBLANES, k)


def _decode_kernel(nchunk, gpages, pt_ref, qa_ref, qr_ref, cnew_ref, krnew_ref, gnew_ref,
                   cache_c, cache_kr, cache_inv, o_ref,
                   raw_c0, raw_c1, raw_kr0, raw_kr1, raw_inv0, raw_inv1, cb0, cb1, s0, s1,
                   qa_scr, qr_scr, m_scr, l_scr, acc_scr, sems):
    b = pl.program_id(0)
    nseq = pl.num_programs(0)
    rows = N_HEADS * SUBLANES
    raws = ((raw_c0, raw_kr0, raw_inv0), (raw_c1, raw_kr1, raw_inv1))
    cbs, ss = (cb0, cb1), (s0, s1)

    def page_copies(seq, chunk, slot):
        raw_c, raw_kr, raw_inv = raws[slot]
        out = []
        for p in range(gpages):
            page = pt_ref[seq, chunk * gpages + p]
            out.append(pltpu.make_async_copy(cache_c.at[page], raw_c.at[p], sems.at[slot, 0]))
            out.append(pltpu.make_async_copy(cache_kr.at[page], raw_kr.at[p], sems.at[slot, 1]))
            out.append(pltpu.make_async_copy(cache_inv.at[page], raw_inv.at[p], sems.at[slot, 2]))
        return out

    def start_chunk(seq, chunk, slot):
        for cp in page_copies(seq, chunk, slot):
            cp.start()

    @pl.when(b == 0)
    def _():
        start_chunk(0, 0, 0)
        start_chunk(0, 1, 1)

    qa_scr[...] = qa_ref[0].reshape(rows, KV_RANK).astype(BF16)
    qr_in = qr_ref[0].reshape(rows, LANES)
    qr_scr[...] = (qr_in[:, :ROPE_DIM] + qr_in[:, ROPE_DIM:]).astype(BF16)
    m_scr[...] = jnp.full(m_scr.shape, NEG_INF, F32)
    l_scr[...] = jnp.zeros(l_scr.shape, F32)
    acc_scr[...] = jnp.zeros(acc_scr.shape, F32)

    def stage_a(chunk, slot):
        raw_c, raw_kr, raw_inv = raws[slot]
        for cp in page_copies(b, chunk, slot):
            cp.wait()
        qa = qa_scr[...]
        qr = qr_scr[...]
        for p in range(gpages):
            cb = raw_c[p].astype(BF16)
            cbs[slot][p * PAGE_SIZE:(p + 1) * PAGE_SIZE, :] = cb
            s_nope = _dot_nt(qa, cb) * _expand_heads(raw_inv[p])
            s_rope = _dot(qr, raw_kr[p].astype(BF16))
            ss[slot][:, p * PAGE_SIZE:(p + 1) * PAGE_SIZE] = s_nope + s_rope
        nxt = chunk + 2
        wraps = nxt >= nchunk
        seq2 = jnp.where(wraps, b + 1, b)
        chunk2 = jnp.where(wraps, nxt - nchunk, nxt)

        @pl.when(seq2 < nseq)
        def _():
            start_chunk(seq2, chunk2, slot)

    def update(s, values_b):
        m_prev = m_scr[...]
        m_new = jnp.maximum(m_prev, jnp.max(s, axis=-1, keepdims=True))
        p = jnp.exp(s - m_new)
        alpha = jnp.exp(m_prev - m_new)
        l_scr[...] = alpha * l_scr[...] + jnp.sum(p, axis=-1, keepdims=True)
        acc_scr[...] = alpha * acc_scr[...] + _dot(p.astype(BF16), values_b)
        m_scr[...] = m_new

    def stage_b(slot):
        update(ss[slot][...], cbs[slot][...])

    stage_a(0, 0)

    def pair(cc, carry):
        stage_a(2 * cc + 1, 1)
        stage_b(0)
        stage_a(2 * cc + 2, 0)
        stage_b(1)
        return carry

    lax.fori_loop(0, (nchunk - 2) // 2, pair, 0)
    stage_a(nchunk - 1, 1)
    stage_b(0)
    stage_b(1)

    t_new = cnew_ref.shape[1]
    pad = lambda a: jnp.concatenate([a, jnp.zeros((PAGE_SIZE - t_new, a.shape[1]), F32)], axis=0).astype(BF16)
    cn = pad(cnew_ref[0])
    s_new = _dot_nt(qa_scr[...], cn) * gnew_ref[0] + _dot_nt(qr_scr[...], pad(krnew_ref[0]))
    tok = lax.broadcasted_iota(jnp.int32, (rows, PAGE_SIZE), 0) % SUBLANES
    key = lax.broadcasted_iota(jnp.int32, (rows, PAGE_SIZE), 1)
    update(jnp.where(key <= tok, s_new, NEG_INF), cn)
    o_ref[0] = acc_scr[...] * (1.0 / l_scr[...])


def _decode_attn(page_table, qa, qr, c_new, kr_new, g_new, cache_c, cache_kr, cache_inv, *, pages_per_chunk):
    nseq, npages = page_table.shape
    t_new = c_new.shape[1]
    gpages = pages_per_chunk
    nchunk = npages // gpages
    assert t_new == SUBLANES and npages % gpages == 0 and nchunk % 2 == 0 and nchunk >= 4
    rows = N_HEADS * t_new
    keys = gpages * PAGE_SIZE
    any_spec = pl.BlockSpec(memory_space=pl.ANY)
    in_specs = [pl.BlockSpec((1, N_HEADS, t_new, KV_RANK), lambda b, pt: (0, 0, b, 0)),
                pl.BlockSpec((1, N_HEADS, t_new, LANES), lambda b, pt: (0, 0, b, 0)),
                pl.BlockSpec((1, t_new, KV_RANK), lambda b, pt: (b, 0, 0)),
                pl.BlockSpec((1, t_new, ROPE_DIM), lambda b, pt: (b, 0, 0)),
                pl.BlockSpec((1, rows, PAGE_SIZE), lambda b, pt: (b, 0, 0)),
                any_spec, any_spec, any_spec]
    two = lambda shape, dtype: [pltpu.VMEM(shape, dtype), pltpu.VMEM(shape, dtype)]
    grid_spec = pltpu.PrefetchScalarGridSpec(
        num_scalar_prefetch=1,
        grid=(nseq,),
        in_specs=in_specs,
        out_specs=pl.BlockSpec((1, rows, KV_RANK), lambda b, pt: (b, 0, 0)),
        scratch_shapes=two((gpages, PAGE_SIZE, KV_RANK), F32) + two((gpages, ROPE_DIM, PAGE_SIZE), F32)
                       + two((gpages, N_HEADS, PAGE_SIZE), F32) + two((keys, KV_RANK), BF16) + two((rows, keys), F32)
                       + [pltpu.VMEM((rows, KV_RANK), BF16), pltpu.VMEM((rows, ROPE_DIM), BF16),
                          pltpu.VMEM((rows, 1), F32), pltpu.VMEM((rows, 1), F32), pltpu.VMEM((rows, KV_RANK), F32),
                          pltpu.SemaphoreType.DMA((2, 3))],
    )
    return pl.pallas_call(
        functools.partial(_decode_kernel, nchunk, gpages),
        grid_spec=grid_spec,
        out_shape=jax.ShapeDtypeStruct((nseq, rows, KV_RANK), F32),
        compiler_params=_params("arbitrary"),
    )(page_table, qa, qr, c_new, kr_new, g_new, cache_c, cache_kr, cache_inv)


def _uv_kernel(o_ref, w_ref, out_ref):
    nseq, _, t, r = o_ref.shape
    o = o_ref[...].reshape(nseq * t, r).astype(BF16)
    out_ref[...] = _dot(o, w_ref[0]).astype(BF16)


def _uv_proj(o_lat4, w_uv_h):
    nseq, h, t, r = o_lat4.shape
    return pl.pallas_call(
        _uv_kernel,
        grid=(h,),
        in_specs=[pl.BlockSpec((nseq, 1, t, r), lambda i: (0, i, 0, 0)),
                  pl.BlockSpec((1, r, V_DIM), lambda i: (i, 0, 0))],
        out_specs=pl.BlockSpec((nseq * t, V_DIM), lambda i: (0, i)),
        out_shape=jax.ShapeDtypeStruct((nseq * t, h * V_DIM), BF16),
        compiler_params=_params("arbitrary"),
    )(o_lat4, w_uv_h)


def _merge_kernel(short_seq, x_ref, attn_ref, b_ref, u_ref, prev_ref, ga_ref, gc_ref, wconv_ref, wo_ref,
                  gffn_ref, h_out, n2_out):
    u = u_ref[...]
    bm, c = u.shape
    if short_seq:
        nseq = bm // SUBLANES
        u3 = u.reshape(nseq, SUBLANES, c)
        tok = lax.broadcasted_iota(jnp.int32, (nseq, SUBLANES, c), 1)
        st = prev_ref[...]
        s0, s1 = st[:, 0:1, :], st[:, 1:2, :]
        u1 = jnp.where(tok == 0, s1, pltpu.roll(u3, 1, axis=1))
        u2 = jnp.where(tok == 0, s0, jnp.where(tok == 1, s1, pltpu.roll(u3, 2, axis=1)))
        u1 = u1.reshape(bm, c)
        u2 = u2.reshape(bm, c)
    else:
        first = pl.program_id(1) == 0
        halo = jnp.where(first, 0.0, prev_ref[0])
        row = lax.broadcasted_iota(jnp.int32, (bm, c), 0)
        h1, h2 = halo[SUBLANES - 1:SUBLANES, :], halo[SUBLANES - 2:SUBLANES - 1, :]
        u1 = jnp.where(row == 0, h1, pltpu.roll(u, 1, axis=0))
        u2 = jnp.where(row == 0, h2, jnp.where(row == 1, h1, pltpu.roll(u, 2, axis=0)))
    wc = wconv_ref[...]
    conv = wc[0:1, :] * u2 + wc[1:2, :] * u1 + wc[2:3, :] * u
    mixed = ga_ref[...].astype(F32) * attn_ref[...].astype(F32) + gc_ref[...].astype(F32) * (b_ref[...].astype(F32) * conv)
    h = x_ref[...] + _dot(mixed.astype(BF16), wo_ref[...])
    h_out[...] = h
    n2_out[...] = (_unit_rms(h) * gffn_ref[...]).astype(BF16)


def _merge(short_seq, x3, attn3, b3, u3, prev, ga3, gc3, w_conv, w_o, g_ffn, *, bm):
    g, r, d = x3.shape
    row = pl.BlockSpec((None, bm, d), lambda b, i: (b, i, 0))
    if short_seq:
        prev_spec = pl.BlockSpec((bm // SUBLANES, CONV_K - 1, d), lambda b, i: (i, 0, 0))
    else:
        per = bm // SUBLANES
        prev_spec = pl.BlockSpec((1, SUBLANES, d), lambda b, i: (b, jnp.maximum(i * per - 1, 0), 0))
    return pl.pallas_call(
        functools.partial(_merge_kernel, short_seq),
        grid=(g, r // bm),
        in_specs=[row, row, row, row, prev_spec, row, row,
                  pl.BlockSpec(w_conv.shape, lambda b, i: (0, 0)),
                  pl.BlockSpec(w_o.shape, lambda b, i: (0, 0), pipeline_mode=pl.Buffered(1)),
                  pl.BlockSpec(g_ffn.shape, lambda b, i: (0, 0))],
        out_specs=[row, row],
        out_shape=[jax.ShapeDtypeStruct((g, r, d), F32), jax.ShapeDtypeStruct((g, r, d), BF16)],
        compiler_params=_params("arbitrary", "arbitrary"),
    )(x3, attn3, b3, u3, prev, ga3, gc3, w_conv, w_o, g_ffn)


def _ffn_kernel(n2_ref, h_ref, wg_ref, wu_ref, wd_ref, y_ref, acc_scr):
    f = pl.program_id(1)

    @pl.when(f == 0)
    def _():
        acc_scr[...] = h_ref[...]

    n2 = n2_ref[...]
    act = (jax.nn.silu(_dot(n2, wg_ref[...])) * _dot(n2, wu_ref[...])).astype(BF16)
    acc_scr[...] += _dot(act, wd_ref[...])

    @pl.when(f == pl.num_programs(1) - 1)
    def _():
        y_ref[...] = acc_scr[...]


def _ffn(n2, h, w_gate_up, w_down, *, bm, bf):
    m, d = h.shape
    d_ff = w_down.shape[0]
    nf = d_ff // bf
    row = pl.BlockSpec((bm, d), lambda i, f: (i, 0))
    return pl.pallas_call(
        _ffn_kernel,
        grid=(m // bm, nf),
        in_specs=[row, row,
                  pl.BlockSpec((d, bf), lambda i, f: (0, f)),
                  pl.BlockSpec((d, bf), lambda i, f: (0, f + nf)),
                  pl.BlockSpec((bf, d), lambda i, f: (f, 0))],
        out_specs=row,
        out_shape=jax.ShapeDtypeStruct((m, d), F32),
        scratch_shapes=[pltpu.VMEM((bm, d), F32)],
        compiler_params=_params("arbitrary", "arbitrary"),
    )(n2, h, w_gate_up, w_gate_up, w_down)


def _rope_tables(pos):
    freqs = ROPE_THETA ** (-jnp.arange(HALF_ROPE, dtype=F32) / HALF_ROPE)
    ang = pos.astype(F32)[:, None] * freqs
    c, s = jnp.cos(ang), jnp.sin(ang)
    return jnp.concatenate([c, c, c, c], axis=-1), jnp.concatenate([-s, s, -s, s], axis=-1)


def _swap_halves(a):
    return jnp.concatenate([a[..., HALF_ROPE:], a[..., :HALF_ROPE]], axis=-1)


def kernel(x_prompt, x_sample, cache_kv_latent, cache_k_rope, cache_k_inv_rms, state_conv, page_table,
           g_attn, w_in, b_gate, g_q_lat, g_kv_lat, w_uq, g_q_nope, g_q_rope, g_k_nope, g_k_rope,
           w_uk, w_uv, w_conv, w_o, g_ffn, w_gate_up, w_down):
    batch, seq, d = x_prompt.shape
    nseq, t_dec, _ = x_sample.shape
    past_len = page_table.shape[1] * PAGE_SIZE
    width = w_conv.shape[1]
    d_ff = w_down.shape[0]

    i1 = Q_RANK + KV_RANK
    i2 = i1 + ROPE_DIM
    i3, i4, i5 = i2 + width, i2 + 2 * width, i2 + 3 * width
    w_lat = jnp.concatenate([w_in[:, :i2], _swap_halves(w_in[:, i1:i2])], axis=1).astype(BF16)
    w_b, w_c, w_x = (w_in[:, a:a + width].astype(BF16) for a in (i2, i3, i4))
    w_ga, w_gc = w_in[:, i5:i5 + d].astype(BF16), w_in[:, i5 + d:].astype(BF16)
    b_ga, b_gc = b_gate[:d].reshape(1, d), b_gate[d:].reshape(1, d)
    uq_rope = w_uq[:, :, NOPE_DIM:]
    w_uq_all = jnp.concatenate([w_uq[:, :, :NOPE_DIM].reshape(Q_RANK, -1), uq_rope.reshape(Q_RANK, -1),
                                _swap_halves(uq_rope).reshape(Q_RANK, -1)], axis=1).astype(BF16)
    w_uk2 = w_uk.reshape(KV_RANK, -1).astype(BF16)
    w_uk_t = jnp.transpose(w_uk, (1, 2, 0)).astype(BF16)
    w_uv_t = jnp.transpose(w_uv, (1, 2, 0)).reshape(N_HEADS * V_DIM, KV_RANK).astype(BF16)
    w_uv_h = jnp.transpose(w_uv, (1, 0, 2)).astype(BF16)
    w_o_b, w_gu_b, w_down_b = w_o.astype(BF16), w_gate_up.astype(BF16), w_down.astype(BF16)
    row = lambda a: a.reshape(1, -1)
    g_kr = row(jnp.concatenate([g_k_rope, _swap_halves(g_k_rope)]))
    g_qr_a = row(jnp.concatenate([g_q_rope, g_q_rope]))
    g_qr_b = row(jnp.concatenate([_swap_halves(g_q_rope)] * 2))
    gains = [row(g_attn), row(g_q_lat), row(g_kv_lat), g_kr, row(g_q_nope), g_qr_a, g_qr_b, row(g_k_nope)]

    bm_lat = 256
    cos_p, sin_p = _rope_tables(jnp.arange(seq))
    cos_s, sin_s = _rope_tables(jnp.tile(past_len + jnp.arange(t_dec), bm_lat // t_dec))

    xp2 = x_prompt.reshape(batch * seq, d)
    b_p, u_p, ga_p, gc_p = _rest_proj(xp2, row(g_attn), w_b, w_c, w_x, w_ga, w_gc, b_ga, b_gc, bm=1024, bn=256)
    qcat, kcat, vt, ckv_p, krope_p, kinv_p = _latent_proj(
        True, x_prompt, cos_p, sin_p, gains, w_lat, w_uq_all, w_uk2, w_uv_t, bm=bm_lat)
    attn_p = _prompt_attn(qcat, kcat, vt, tq=512, heads_per_step=2)
    to3 = lambda a: a.reshape(batch, seq, -1)
    u_p3 = to3(u_p)
    h_p, n2_p = _merge(False, x_prompt, attn_p, to3(b_p), u_p3, u_p3, to3(ga_p), to3(gc_p), w_conv, w_o_b,
                       row(g_ffn), bm=256)
    y_prompt = _ffn(n2_p.reshape(-1, d), h_p.reshape(-1, d), w_gu_b, w_down_b, bm=512, bf=512).reshape(batch, seq, d)
    conv_state_prompt = u_p3[:, seq - (CONV_K - 1):, :]

    m_s = nseq * t_dec
    xs2 = x_sample.reshape(m_s, d)
    b_s, u_s, ga_s, gc_s = _rest_proj(xs2, row(g_attn), w_b, w_c, w_x, w_ga, w_gc, b_ga, b_gc, bm=1024, bn=256)
    qa, qr, ckv_s, krope_s, kinv_s = _latent_proj(
        False, xs2.reshape(1, m_s, d), cos_s, sin_s, gains, w_lat, w_uq_all, w_uk2, w_uk_t, bm=bm_lat)
    ckv_s = ckv_s.reshape(nseq, t_dec, KV_RANK)
    krope_s = krope_s.reshape(nseq, t_dec, ROPE_DIM)
    kinv_s = kinv_s.reshape(nseq, t_dec, N_HEADS)
    g_new = jnp.repeat(jnp.swapaxes(kinv_s, 1, 2), t_dec, axis=1)
    g_new = jnp.concatenate([g_new, jnp.ones((nseq, N_HEADS * t_dec, PAGE_SIZE - t_dec), F32)], axis=-1)
    cache_kr_t = jnp.swapaxes(cache_k_rope, 1, 2)
    cache_inv_t = jnp.swapaxes(cache_k_inv_rms, 1, 2)
    o_lat = _decode_attn(page_table, qa, qr, ckv_s, krope_s, g_new, cache_kv_latent, cache_kr_t, cache_inv_t,
                         pages_per_chunk=16)
    attn_s = _uv_proj(o_lat.reshape(nseq, N_HEADS, t_dec, KV_RANK), w_uv_h)
    to1 = lambda a: a.reshape(1, m_s, -1)
    h_s, n2_s = _merge(True, to1(xs2), to1(attn_s), to1(b_s), to1(u_s), state_conv.astype(F32), to1(ga_s),
                       to1(gc_s), w_conv, w_o_b, row(g_ffn), bm=256)
    y_sample = _ffn(n2_s.reshape(-1, d), h_s.reshape(-1, d), w_gu_b, w_down_b, bm=512, bf=512).reshape(nseq, t_dec, d)
    u_s3 = u_s.reshape(nseq, t_dec, width)
    conv_state_sample = u_s3[:, t_dec - (CONV_K - 1):, :]

    return (y_prompt, y_sample, ckv_p, krope_p, kinv_p, conv_state_prompt,
            ckv_s, krope_s, kinv_s, conv_state_sample)
```

```python
import functools

import jax
import jax.numpy as jnp
from jax import lax
from jax.experimental import pallas as pl
from jax.experimental.pallas import tpu as pltpu

N_HEADS = 16
Q_RANK = 512
KV_RANK = 512
NOPE_DIM = 128
ROPE_DIM = 64
HALF_ROPE = ROPE_DIM // 2
V_DIM = 128
QK_DIM = NOPE_DIM + ROPE_DIM
QK_PAD = 256
ROPE_THETA = 10000.0
ATTN_SCALE = QK_DIM ** -0.5
LOG2_E = 1.4426950408889634
PAGE_SIZE = 128
RAW_SLOTS = 3
CONV_K = 3
EPS = 1e-6
NEG_INF = -1e30
LANES = 128
SUBLANES = 8
VMEM_LIMIT = 56 * 1024 * 1024

F32 = jnp.float32
BF16 = jnp.bfloat16
NT_DIMS = (((1,), (1,)), ((), ()))


def _params(*sem):
    return pltpu.CompilerParams(dimension_semantics=sem, vmem_limit_bytes=VMEM_LIMIT)


def _unit_rms(xf):
    return xf * lax.rsqrt(jnp.mean(xf * xf, axis=-1, keepdims=True) + EPS)


def _dot(a, b):
    return jnp.dot(a, b, preferred_element_type=F32)


def _dot_nt(a, b):
    return lax.dot_general(a, b, NT_DIMS, preferred_element_type=F32)


def _rest_kernel(x_ref, g_ref, wb_ref, wc_ref, wx_ref, wga_ref, wgc_ref, bga_ref, bgc_ref,
                 b_out, u_out, ga_out, gc_out, n_scr):
    @pl.when(pl.program_id(1) == 0)
    def _():
        n_scr[...] = (_unit_rms(x_ref[...]) * g_ref[...]).astype(BF16)

    n = n_scr[...]
    d_b = _dot(n, wb_ref[...])
    d_c = _dot(n, wc_ref[...])
    b_out[...] = d_b.astype(BF16)
    d_x = _dot(n, wx_ref[...])
    d_ga = _dot(n, wga_ref[...])
    u_out[...] = d_c * d_x
    d_gc = _dot(n, wgc_ref[...])
    ga_out[...] = jax.nn.sigmoid(d_ga + bga_ref[...]).astype(BF16)
    gc_out[...] = jax.nn.sigmoid(d_gc + bgc_ref[...]).astype(BF16)


def _rest_proj(x2, g_attn, w_b, w_c, w_x, w_ga, w_gc, b_ga, b_gc, *, bm, bn):
    m, d = x2.shape
    n = w_b.shape[1]
    row = lambda i, j: (i, 0)
    col = lambda i, j: (0, j)
    tile = lambda i, j: (i, j)
    w_spec = pl.BlockSpec((d, bn), col)
    v_spec = pl.BlockSpec((1, bn), col)
    o_spec = pl.BlockSpec((bm, bn), tile)
    return pl.pallas_call(
        _rest_kernel,
        grid=(m // bm, n // bn),
        in_specs=[pl.BlockSpec((bm, d), row), pl.BlockSpec((1, d), lambda i, j: (0, 0)),
                  w_spec, w_spec, w_spec, w_spec, w_spec, v_spec, v_spec],
        out_specs=[o_spec, o_spec, o_spec, o_spec],
        out_shape=[jax.ShapeDtypeStruct((m, n), BF16), jax.ShapeDtypeStruct((m, n), F32),
                   jax.ShapeDtypeStruct((m, n), BF16), jax.ShapeDtypeStruct((m, n), BF16)],
        scratch_shapes=[pltpu.VMEM((bm, d), BF16)],
        compiler_params=_params("arbitrary", "arbitrary"),
    )(x2, g_attn, w_b, w_c, w_x, w_ga, w_gc, b_ga, b_gc)


def _latent_kernel(prompt, x_ref, cos_ref, sin_ref, g_attn, g_q_lat, g_kv_lat, g_kr, g_q_nope, g_qr_a,
                   g_qr_b, g_k_nope, w_lat, w_uq, w_uk, w_aux, *outs):
    if prompt:
        qcat_out, kcat_out, vt_out, ckv_out, krope_out, kinv_out = outs
    else:
        qa_out, qr_out, ckv_out, krope_out, kinv_out = outs
    bm = x_ref.shape[1]
    cos_t = cos_ref[...]
    sin_t = sin_ref[...]
    lane = lax.broadcasted_iota(jnp.int32, (bm, LANES), 1)
    low_half = lane < ROPE_DIM
    q_scale = ATTN_SCALE * LOG2_E if prompt else ATTN_SCALE

    n = (_unit_rms(x_ref[0]) * g_attn[...]).astype(BF16)
    lat = _dot(n, w_lat[...])
    cq = (_unit_rms(lat[:, :Q_RANK]) * g_q_lat[...]).astype(BF16)
    ckv = _unit_rms(lat[:, Q_RANK:Q_RANK + KV_RANK]) * g_kv_lat[...]
    ckv_out[0] = ckv
    ckv_b = ckv.astype(BF16)

    krp = lat[:, Q_RANK + KV_RANK:]
    kr_ms = jnp.sum(jnp.where(low_half, krp * krp, 0.0), axis=-1, keepdims=True) * (1.0 / ROPE_DIM)
    krp = krp * lax.rsqrt(kr_ms + EPS) * g_kr[...]
    krt = krp * jnp.where(low_half, cos_t, sin_t)
    kr2 = krt + pltpu.roll(krt, ROPE_DIM, axis=1)
    krope_out[0] = kr2[:, :ROPE_DIM]

    kraw = _dot(ckv_b, w_uk[...])
    kinv = jnp.zeros((bm, LANES), F32)
    for h in range(N_HEADS):
        kh = kraw[:, h * NOPE_DIM:(h + 1) * NOPE_DIM]
        inv_h = lax.rsqrt(jnp.mean(kh * kh, axis=-1, keepdims=True) + EPS)
        kinv = jnp.where(lane == h, inv_h, kinv)
        if prompt:
            kcat_out[0, h, :, :NOPE_DIM] = (kh * inv_h * g_k_nope[...]).astype(BF16)
            kcat_out[0, h, :, NOPE_DIM:] = kr2.astype(BF16)
    kinv_out[0] = kinv[:, :N_HEADS]

    if prompt:
        vt = _dot_nt(w_aux[...], ckv_b)
        vt_out[0, :, 0] = vt.reshape(N_HEADS, V_DIM, bm).astype(BF16)

    q = _dot(cq, w_uq[...])
    rope_a0 = N_HEADS * NOPE_DIM
    rope_b0 = rope_a0 + N_HEADS * ROPE_DIM
    for h in range(N_HEADS):
        qh = _unit_rms(q[:, h * NOPE_DIM:(h + 1) * NOPE_DIM]) * (g_q_nope[...] * q_scale)
        if prompt:
            qcat_out[0, h, :, :NOPE_DIM] = qh.astype(BF16)
        else:
            qa_out[0, h] = _dot((qh * g_k_nope[...]).astype(BF16), w_aux[h])
    for p in range(N_HEADS // 2):
        qa_ = q[:, rope_a0 + p * LANES: rope_a0 + (p + 1) * LANES]
        qb_ = q[:, rope_b0 + p * LANES: rope_b0 + (p + 1) * LANES]
        sq = qa_ * qa_
        ms_lo = jnp.sum(jnp.where(low_half, sq, 0.0), axis=-1, keepdims=True) * (1.0 / ROPE_DIM)
        ms_hi = jnp.sum(jnp.where(low_half, 0.0, sq), axis=-1, keepdims=True) * (1.0 / ROPE_DIM)
        inv = jnp.where(low_half, lax.rsqrt(ms_lo + EPS), lax.rsqrt(ms_hi + EPS)) * q_scale
        rot = (qa_ * g_qr_a[...] * cos_t + qb_ * g_qr_b[...] * sin_t) * inv
        even = jnp.where(low_half, rot, 0.0)
        odd = jnp.where(low_half, 0.0, rot)
        if prompt:
            qcat_out[0, 2 * p, :, NOPE_DIM:] = even.astype(BF16)
            qcat_out[0, 2 * p + 1, :, NOPE_DIM:] = odd.astype(BF16)
        else:
            qr_out[0, 2 * p] = even
            qr_out[0, 2 * p + 1] = odd


def _latent_proj(prompt, x3, cos_t, sin_t, gains, w_lat, w_uq, w_uk, w_aux, *, bm):
    g, r, d = x3.shape
    nblk = r // bm
    tab_blocks = cos_t.shape[0] // bm
    const2 = lambda a: pl.BlockSpec(a.shape, lambda b, i: (0, 0), pipeline_mode=pl.Buffered(1))
    const3 = lambda a: pl.BlockSpec(a.shape, lambda b, i: (0, 0, 0), pipeline_mode=pl.Buffered(1))
    tab_spec = pl.BlockSpec((bm, LANES), lambda b, i: (i % tab_blocks, 0))
    row3 = lambda w: pl.BlockSpec((1, bm, w), lambda b, i: (b, i, 0))
    head4 = lambda w: pl.BlockSpec((1, N_HEADS, bm, w), lambda b, i: (b, 0, i, 0))
    if prompt:
        out_specs = [head4(QK_PAD), head4(QK_PAD),
                     pl.BlockSpec((1, N_HEADS, 1, V_DIM, bm), lambda b, i: (b, 0, i, 0, 0)),
                     row3(KV_RANK), row3(ROPE_DIM), row3(N_HEADS)]
        out_shape = [jax.ShapeDtypeStruct((g, N_HEADS, r, QK_PAD), BF16),
                     jax.ShapeDtypeStruct((g, N_HEADS, r, QK_PAD), BF16),
                     jax.ShapeDtypeStruct((g, N_HEADS, nblk, V_DIM, bm), BF16)]
    else:
        out_specs = [head4(KV_RANK), head4(LANES), row3(KV_RANK), row3(ROPE_DIM), row3(N_HEADS)]
        out_shape = [jax.ShapeDtypeStruct((g, N_HEADS, r, KV_RANK), F32),
                     jax.ShapeDtypeStruct((g, N_HEADS, r, LANES), F32)]
    out_shape += [jax.ShapeDtypeStruct((g, r, KV_RANK), F32), jax.ShapeDtypeStruct((g, r, ROPE_DIM), F32),
                  jax.ShapeDtypeStruct((g, r, N_HEADS), F32)]
    aux_spec = const2(w_aux) if prompt else const3(w_aux)
    return pl.pallas_call(
        functools.partial(_latent_kernel, prompt),
        grid=(g, nblk),
        in_specs=[pl.BlockSpec((1, bm, d), lambda b, i: (b, i, 0)), tab_spec, tab_spec]
                 + [const2(a) for a in gains] + [const2(w_lat), const2(w_uq), const2(w_uk), aux_spec],
        out_specs=out_specs,
        out_shape=out_shape,
        compiler_params=_params("arbitrary", "arbitrary"),
    )(x3, cos_t, sin_t, *gains, w_lat, w_uq, w_uk, w_aux)


def _prompt_attn_kernel(q_ref, k_ref, vt_ref, o_ref, st0, st1, cm0, cm1, m_scr, l_scr, acc_scr, *, tq, vb, hps):
    qi = pl.program_id(2)
    sts, cms = (st0, st1), (cm0, cm1)
    nv = tq // vb
    m_scr[...] = jnp.full(m_scr.shape, NEG_INF, F32)
    l_scr[...] = jnp.zeros(l_scr.shape, F32)
    acc_scr[...] = jnp.zeros(acc_scr.shape, F32)

    def q_phase(jk, slot, masked):
        for hh in range(hps):
            k = k_ref[0, hh, pl.ds(pl.multiple_of(jk * tq, tq), tq), :]
            st = _dot_nt(k, q_ref[0, hh])
            if masked:
                kpos = lax.broadcasted_iota(jnp.int32, (tq, tq), 0)
                qpos = lax.broadcasted_iota(jnp.int32, (tq, tq), 1)
                st = jnp.where(kpos <= qpos, st, NEG_INF)
            sts[slot][hh] = st
            cms[slot][hh] = jnp.max(st, axis=0, keepdims=True)

    def sp_phase(jv, slot):
        for hh in range(hps):
            m_prev = m_scr[hh]
            m_new = jnp.maximum(m_prev, cms[slot][hh])
            alpha = jnp.exp2(m_prev - m_new)
            p = jnp.exp2(sts[slot][hh] - m_new)
            l_scr[hh] = alpha * l_scr[hh] + jnp.sum(p, axis=0, keepdims=True)
            pb = p.astype(BF16)
            pv = _dot(vt_ref[0, hh, jv * nv], pb[:vb])
            for s in range(1, nv):
                pv += _dot(vt_ref[0, hh, jv * nv + s], pb[s * vb:(s + 1) * vb])
            acc_scr[hh] = alpha * acc_scr[hh] + pv
            m_scr[hh] = m_new

    q_phase(qi, 0, True)

    def two_visits(i, carry):
        t = 2 * i
        q_phase(t, 1, False)
        sp_phase(jnp.where(t == 0, qi, t - 1), 0)
        q_phase(t + 1, 0, False)
        sp_phase(t, 1)
        return carry

    lax.fori_loop(0, qi // 2, two_visits, 0)

    @pl.when(qi % 2 == 1)
    def _():
        t = qi - 1
        q_phase(t, 1, False)
        sp_phase(jnp.where(t == 0, qi, t - 1), 0)
        sp_phase(t, 1)

    @pl.when(qi % 2 == 0)
    def _():
        sp_phase(jnp.where(qi == 0, qi, qi - 1), 0)

    for hh in range(hps):
        o = acc_scr[hh] * (1.0 / l_scr[hh])
        o_ref[0, :, hh * V_DIM:(hh + 1) * V_DIM] = o.T.astype(BF16)


def _prompt_attn(qcat, kcat, vt, *, tq, heads_per_step):
    b, h, s, w = qcat.shape
    nvb, vb = vt.shape[2], vt.shape[4]
    hps = heads_per_step
    return pl.pallas_call(
        functools.partial(_prompt_attn_kernel, tq=tq, vb=vb, hps=hps),
        grid=(b, h // hps, s // tq),
        in_specs=[pl.BlockSpec((1, hps, tq, w), lambda bi, hi, qi: (bi, hi, qi, 0)),
                  pl.BlockSpec((1, hps, s, w), lambda bi, hi, qi: (bi, hi, 0, 0)),
                  pl.BlockSpec((1, hps, nvb, V_DIM, vb), lambda bi, hi, qi: (bi, hi, 0, 0, 0))],
        out_specs=pl.BlockSpec((1, tq, hps * V_DIM), lambda bi, hi, qi: (bi, qi, hi)),
        out_shape=jax.ShapeDtypeStruct((b, s, h * V_DIM), BF16),
        scratch_shapes=[pltpu.VMEM((hps, tq, tq), F32), pltpu.VMEM((hps, tq, tq), F32),
                        pltpu.VMEM((hps, 1, tq), F32), pltpu.VMEM((hps, 1, tq), F32),
                        pltpu.VMEM((hps, 1, tq), F32), pltpu.VMEM((hps, 1, tq), F32),
                        pltpu.VMEM((hps, V_DIM, tq), F32)],
        compiler_params=_params("arbitrary", "arbitrary", "arbitrary"),
    )(qcat, kcat, vt)


def _expand_heads(inv_t):
    k = inv_t.shape[1]
    return jnp.broadcast_to(inv_t[:, None, :], (N_HEADS, SUBLANES, k)).reshape(N_HEADS * SUBLANES, k)


def _decode_kernel(nchunk, gpages, pt_ref, qa_ref, qr_ref, cnew_ref, krnew_ref, gnew_ref,
                   cache_c, cache_kr, cache_inv, o_ref,
                   raw_c, raw_kr, raw_inv, cb0, cb1, s0, s1,
                   qa_scr, qr_scr, m_scr, l_scr, acc_scr, sems):
    b = pl.program_id(0)
    nseq = pl.num_programs(0)
    rows = N_HEADS * SUBLANES
    cbs, ss = (cb0, cb1), (s0, s1)
    raw_bufs = (raw_c, raw_kr, raw_inv)
    caches = (cache_c, cache_kr, cache_inv)

    def start_chunk(seq, chunk, r):
        for p in range(gpages):
            page = pt_ref[seq, chunk * gpages + p]
            for a in range(3):
                pltpu.make_async_copy(caches[a].at[page], raw_bufs[a].at[r, p], sems.at[r, a]).start()

    def wait_chunk(r):
        for a in range(3):
            pltpu.make_async_copy(caches[a].at[pl.ds(0, gpages)], raw_bufs[a].at[r], sems.at[r, a]).wait()

    @pl.when(b == 0)
    def _():
        for g in range(RAW_SLOTS):
            start_chunk(0, g, g)

    qa_scr[...] = qa_ref[0].reshape(rows, KV_RANK).astype(BF16)
    qr_in = qr_ref[0].reshape(rows, LANES)
    qr_scr[...] = (qr_in[:, :ROPE_DIM] + qr_in[:, ROPE_DIM:]).astype(BF16)
    m_scr[...] = jnp.full(m_scr.shape, NEG_INF, F32)
    l_scr[...] = jnp.zeros(l_scr.shape, F32)
    acc_scr[...] = jnp.zeros(acc_scr.shape, F32)

    def stage_a(chunk, slot):
        r = lax.rem(b * nchunk + chunk, RAW_SLOTS)
        wait_chunk(r)
        qa = qa_scr[...]
        qr = qr_scr[...]
        for p in range(gpages):
            cb = raw_c[r, p].astype(BF16)
            cbs[slot][p * PAGE_SIZE:(p + 1) * PAGE_SIZE, :] = cb
            s_nope = _dot_nt(qa, cb) * _expand_heads(raw_inv[r, p])
            s_rope = _dot(qr, raw_kr[r, p].astype(BF16))
            ss[slot][:, p * PAGE_SIZE:(p + 1) * PAGE_SIZE] = s_nope + s_rope
        nxt = chunk + RAW_SLOTS
        wraps = nxt >= nchunk
        seq2 = jnp.where(wraps, b + 1, b)
        chunk2 = jnp.where(wraps, nxt - nchunk, nxt)

        @pl.when(seq2 < nseq)
        def _():
            start_chunk(seq2, chunk2, r)

    def update(s, values_b):
        m_prev = m_scr[...]
        m_new = jnp.maximum(m_prev, jnp.max(s, axis=-1, keepdims=True))
        p = jnp.exp(s - m_new)
        alpha = jnp.exp(m_prev - m_new)
        l_scr[...] = alpha * l_scr[...] + jnp.sum(p, axis=-1, keepdims=True)
        acc_scr[...] = alpha * acc_scr[...] + _dot(p.astype(BF16), values_b)
        m_scr[...] = m_new

    def stage_b(slot):
        update(ss[slot][...], cbs[slot][...])

    stage_a(0, 0)

    def pair(cc, carry):
        stage_a(2 * cc + 1, 1)
        stage_b(0)
        stage_a(2 * cc + 2, 0)
        stage_b(1)
        return carry

    lax.fori_loop(0, (nchunk - 2) // 2, pair, 0)
    stage_a(nchunk - 1, 1)
    stage_b(0)
    stage_b(1)

    t_new = cnew_ref.shape[1]
    pad = lambda a: jnp.concatenate([a, jnp.zeros((PAGE_SIZE - t_new, a.shape[1]), F32)], axis=0).astype(BF16)
    cn = pad(cnew_ref[0])
    s_new = _dot_nt(qa_scr[...], cn) * gnew_ref[0] + _dot_nt(qr_scr[...], pad(krnew_ref[0]))
    tok = lax.broadcasted_iota(jnp.int32, (rows, PAGE_SIZE), 0) % SUBLANES
    key = lax.broadcasted_iota(jnp.int32, (rows, PAGE_SIZE), 1)
    update(jnp.where(key <= tok, s_new, NEG_INF), cn)
    o_ref[0] = acc_scr[...] * (1.0 / l_scr[...])


def _decode_attn(page_table, qa, qr, c_new, kr_new, g_new, cache_c, cache_kr, cache_inv, *, pages_per_chunk):
    nseq, npages = page_table.shape
    t_new = c_new.shape[1]
    gpages = pages_per_chunk
    nchunk = npages // gpages
    assert t_new == SUBLANES and npages % gpages == 0 and nchunk % 2 == 0 and nchunk > RAW_SLOTS
    rows = N_HEADS * t_new
    keys = gpages * PAGE_SIZE
    any_spec = pl.BlockSpec(memory_space=pl.ANY)
    in_specs = [pl.BlockSpec((1, N_HEADS, t_new, KV_RANK), lambda b, pt: (0, 0, b, 0)),
                pl.BlockSpec((1, N_HEADS, t_new, LANES), lambda b, pt: (0, 0, b, 0)),
                pl.BlockSpec((1, t_new, KV_RANK), lambda b, pt: (b, 0, 0)),
                pl.BlockSpec((1, t_new, ROPE_DIM), lambda b, pt: (b, 0, 0)),
                pl.BlockSpec((1, rows, PAGE_SIZE), lambda b, pt: (b, 0, 0)),
                any_spec, any_spec, any_spec]
    two = lambda shape, dtype: [pltpu.VMEM(shape, dtype), pltpu.VMEM(shape, dtype)]
    grid_spec = pltpu.PrefetchScalarGridSpec(
        num_scalar_prefetch=1,
        grid=(nseq,),
        in_specs=in_specs,
        out_specs=pl.BlockSpec((1, rows, KV_RANK), lambda b, pt: (b, 0, 0)),
        scratch_shapes=[pltpu.VMEM((RAW_SLOTS, gpages, PAGE_SIZE, KV_RANK), F32),
                        pltpu.VMEM((RAW_SLOTS, gpages, ROPE_DIM, PAGE_SIZE), F32),
                        pltpu.VMEM((RAW_SLOTS, gpages, N_HEADS, PAGE_SIZE), F32)]
                       + two((keys, KV_RANK), BF16) + two((rows, keys), F32)
                       + [pltpu.VMEM((rows, KV_RANK), BF16), pltpu.VMEM((rows, ROPE_DIM), BF16),
                          pltpu.VMEM((rows, 1), F32), pltpu.VMEM((rows, 1), F32), pltpu.VMEM((rows, KV_RANK), F32),
                          pltpu.SemaphoreType.DMA((RAW_SLOTS, 3))],
    )
    return pl.pallas_call(
        functools.partial(_decode_kernel, nchunk, gpages),
        grid_spec=grid_spec,
        out_shape=jax.ShapeDtypeStruct((nseq, rows, KV_RANK), F32),
        compiler_params=_params("arbitrary"),
    )(page_table, qa, qr, c_new, kr_new, g_new, cache_c, cache_kr, cache_inv)


def _uv_kernel(o_ref, w_ref, out_ref):
    nseq, _, t, r = o_ref.shape
    o = o_ref[...].reshape(nseq * t, r).astype(BF16)
    out_ref[...] = _dot(o, w_ref[0]).astype(BF16)


def _uv_proj(o_lat4, w_uv_h):
    nseq, h, t, r = o_lat4.shape
    return pl.pallas_call(
        _uv_kernel,
        grid=(h,),
        in_specs=[pl.BlockSpec((nseq, 1, t, r), lambda i: (0, i, 0, 0)),
                  pl.BlockSpec((1, r, V_DIM), lambda i: (i, 0, 0))],
        out_specs=pl.BlockSpec((nseq * t, V_DIM), lambda i: (0, i)),
        out_shape=jax.ShapeDtypeStruct((nseq * t, h * V_DIM), BF16),
        compiler_params=_params("arbitrary"),
    )(o_lat4, w_uv_h)


def _merge_kernel(short_seq, x_ref, attn_ref, b_ref, u_ref, prev_ref, ga_ref, gc_ref, wconv_ref, wo_ref,
                  gffn_ref, h_out, n2_out):
    u = u_ref[...]
    bm, c = u.shape
    if short_seq:
        nseq = bm // SUBLANES
        u3 = u.reshape(nseq, SUBLANES, c)
        tok = lax.broadcasted_iota(jnp.int32, (nseq, SUBLANES, c), 1)
        st = prev_ref[...]
        s0, s1 = st[:, 0:1, :], st[:, 1:2, :]
        u1 = jnp.where(tok == 0, s1, pltpu.roll(u3, 1, axis=1))
        u2 = jnp.where(tok == 0, s0, jnp.where(tok == 1, s1, pltpu.roll(u3, 2, axis=1)))
        u1 = u1.reshape(bm, c)
        u2 = u2.reshape(bm, c)
    else:
        first = pl.program_id(1) == 0
        halo = jnp.where(first, 0.0, prev_ref[0])
        row = lax.broadcasted_iota(jnp.int32, (bm, c), 0)
        h1, h2 = halo[SUBLANES - 1:SUBLANES, :], halo[SUBLANES - 2:SUBLANES - 1, :]
        u1 = jnp.where(row == 0, h1, pltpu.roll(u, 1, axis=0))
        u2 = jnp.where(row == 0, h2, jnp.where(row == 1, h1, pltpu.roll(u, 2, axis=0)))
    wc = wconv_ref[...]
    conv = wc[0:1, :] * u2 + wc[1:2, :] * u1 + wc[2:3, :] * u
    mixed = ga_ref[...].astype(F32) * attn_ref[...].astype(F32) + gc_ref[...].astype(F32) * (b_ref[...].astype(F32) * conv)
    h = x_ref[...] + _dot(mixed.astype(BF16), wo_ref[...])
    h_out[...] = h
    n2_out[...] = (_unit_rms(h) * gffn_ref[...]).astype(BF16)


def _merge(short_seq, x3, attn3, b3, u3, prev, ga3, gc3, w_conv, w_o, g_ffn, *, bm):
    g, r, d = x3.shape
    row = pl.BlockSpec((None, bm, d), lambda b, i: (b, i, 0))
    if short_seq:
        prev_spec = pl.BlockSpec((bm // SUBLANES, CONV_K - 1, d), lambda b, i: (i, 0, 0))
    else:
        per = bm // SUBLANES
        prev_spec = pl.BlockSpec((1, SUBLANES, d), lambda b, i: (b, jnp.maximum(i * per - 1, 0), 0))
    return pl.pallas_call(
        functools.partial(_merge_kernel, short_seq),
        grid=(g, r // bm),
        in_specs=[row, row, row, row, prev_spec, row, row,
                  pl.BlockSpec(w_conv.shape, lambda b, i: (0, 0)),
                  pl.BlockSpec(w_o.shape, lambda b, i: (0, 0), pipeline_mode=pl.Buffered(1)),
                  pl.BlockSpec(g_ffn.shape, lambda b, i: (0, 0))],
        out_specs=[row, row],
        out_shape=[jax.ShapeDtypeStruct((g, r, d), F32), jax.ShapeDtypeStruct((g, r, d), BF16)],
        compiler_params=_params("arbitrary", "arbitrary"),
    )(x3, attn3, b3, u3, prev, ga3, gc3, w_conv, w_o, g_ffn)


def _ffn_kernel(n2_ref, h_ref, wg_ref, wu_ref, wd_ref, y_ref):
    @pl.when(pl.program_id(1) == 0)
    def _():
        y_ref[...] = h_ref[...]

    n2 = n2_ref[...]
    act = (jax.nn.silu(_dot(n2, wg_ref[...])) * _dot(n2, wu_ref[...])).astype(BF16)
    y_ref[...] += _dot(act, wd_ref[...])


def _ffn(n2, h, w_gate_up, w_down, *, bm, bf):
    m, d = h.shape
    d_ff = w_down.shape[0]
    nf = d_ff // bf
    row = pl.BlockSpec((bm, d), lambda i, f: (i, 0))
    return pl.pallas_call(
        _ffn_kernel,
        grid=(m // bm, nf),
        in_specs=[row,
                  pl.BlockSpec((bm, d), lambda i, f: (i, 0), pipeline_mode=pl.Buffered(1)),
                  pl.BlockSpec((d, bf), lambda i, f: (0, f)),
                  pl.BlockSpec((d, bf), lambda i, f: (0, f + nf)),
                  pl.BlockSpec((bf, d), lambda i, f: (f, 0))],
        out_specs=row,
        out_shape=jax.ShapeDtypeStruct((m, d), F32),
        compiler_params=_params("arbitrary", "arbitrary"),
    )(n2, h, w_gate_up, w_gate_up, w_down)


def _rope_tables(pos):
    freqs = ROPE_THETA ** (-jnp.arange(HALF_ROPE, dtype=F32) / HALF_ROPE)
    ang = pos.astype(F32)[:, None] * freqs
    c, s = jnp.cos(ang), jnp.sin(ang)
    return jnp.concatenate([c, c, c, c], axis=-1), jnp.concatenate([-s, s, -s, s], axis=-1)


def _swap_halves(a):
    return jnp.concatenate([a[..., HALF_ROPE:], a[..., :HALF_ROPE]], axis=-1)


def kernel(x_prompt, x_sample, cache_kv_latent, cache_k_rope, cache_k_inv_rms, state_conv, page_table,
           g_attn, w_in, b_gate, g_q_lat, g_kv_lat, w_uq, g_q_nope, g_q_rope, g_k_nope, g_k_rope,
           w_uk, w_uv, w_conv, w_o, g_ffn, w_gate_up, w_down):
    batch, seq, d = x_prompt.shape
    nseq, t_dec, _ = x_sample.shape
    past_len = page_table.shape[1] * PAGE_SIZE
    width = w_conv.shape[1]
    d_ff = w_down.shape[0]

    i1 = Q_RANK + KV_RANK
    i2 = i1 + ROPE_DIM
    i3, i4, i5 = i2 + width, i2 + 2 * width, i2 + 3 * width
    w_lat = jnp.concatenate([w_in[:, :i2], _swap_halves(w_in[:, i1:i2])], axis=1).astype(BF16)
    w_b, w_c, w_x = (w_in[:, a:a + width].astype(BF16) for a in (i2, i3, i4))
    w_ga, w_gc = w_in[:, i5:i5 + d].astype(BF16), w_in[:, i5 + d:].astype(BF16)
    b_ga, b_gc = b_gate[:d].reshape(1, d), b_gate[d:].reshape(1, d)
    uq_rope = w_uq[:, :, NOPE_DIM:]
    w_uq_all = jnp.concatenate([w_uq[:, :, :NOPE_DIM].reshape(Q_RANK, -1), uq_rope.reshape(Q_RANK, -1),
                                _swap_halves(uq_rope).reshape(Q_RANK, -1)], axis=1).astype(BF16)
    w_uk2 = w_uk.reshape(KV_RANK, -1).astype(BF16)
    w_uk_t = jnp.transpose(w_uk, (1, 2, 0)).astype(BF16)
    w_uv_t = jnp.transpose(w_uv, (1, 2, 0)).reshape(N_HEADS * V_DIM, KV_RANK).astype(BF16)
    w_uv_h = jnp.transpose(w_uv, (1, 0, 2)).astype(BF16)
    w_o_b, w_gu_b, w_down_b = w_o.astype(BF16), w_gate_up.astype(BF16), w_down.astype(BF16)
    row = lambda a: a.reshape(1, -1)
    g_kr = row(jnp.concatenate([g_k_rope, _swap_halves(g_k_rope)]))
    g_qr_a = row(jnp.concatenate([g_q_rope, g_q_rope]))
    g_qr_b = row(jnp.concatenate([_swap_halves(g_q_rope)] * 2))
    gains = [row(g_attn), row(g_q_lat), row(g_kv_lat), g_kr, row(g_q_nope), g_qr_a, g_qr_b, row(g_k_nope)]

    bm_lat = 256
    cos_p, sin_p = _rope_tables(jnp.arange(seq))
    cos_s, sin_s = _rope_tables(jnp.tile(past_len + jnp.arange(t_dec), bm_lat // t_dec))

    xp2 = x_prompt.reshape(batch * seq, d)
    b_p, u_p, ga_p, gc_p = _rest_proj(xp2, row(g_attn), w_b, w_c, w_x, w_ga, w_gc, b_ga, b_gc, bm=1024, bn=256)
    qcat, kcat, vt, ckv_p, krope_p, kinv_p = _latent_proj(
        True, x_prompt, cos_p, sin_p, gains, w_lat, w_uq_all, w_uk2, w_uv_t, bm=bm_lat)
    attn_p = _prompt_attn(qcat, kcat, vt, tq=512, heads_per_step=2)
    to3 = lambda a: a.reshape(batch, seq, -1)
    u_p3 = to3(u_p)
    h_p, n2_p = _merge(False, x_prompt, attn_p, to3(b_p), u_p3, u_p3, to3(ga_p), to3(gc_p), w_conv, w_o_b,
                       row(g_ffn), bm=256)
    y_prompt = _ffn(n2_p.reshape(-1, d), h_p.reshape(-1, d), w_gu_b, w_down_b, bm=1024, bf=512).reshape(batch, seq, d)
    conv_state_prompt = u_p3[:, seq - (CONV_K - 1):, :]

    m_s = nseq * t_dec
    xs2 = x_sample.reshape(m_s, d)
    b_s, u_s, ga_s, gc_s = _rest_proj(xs2, row(g_attn), w_b, w_c, w_x, w_ga, w_gc, b_ga, b_gc, bm=1024, bn=256)
    qa, qr, ckv_s, krope_s, kinv_s = _latent_proj(
        False, xs2.reshape(1, m_s, d), cos_s, sin_s, gains, w_lat, w_uq_all, w_uk2, w_uk_t, bm=bm_lat)
    ckv_s = ckv_s.reshape(nseq, t_dec, KV_RANK)
    krope_s = krope_s.reshape(nseq, t_dec, ROPE_DIM)
    kinv_s = kinv_s.reshape(nseq, t_dec, N_HEADS)
    g_new = jnp.repeat(jnp.swapaxes(kinv_s, 1, 2), t_dec, axis=1)
    g_new = jnp.concatenate([g_new, jnp.ones((nseq, N_HEADS * t_dec, PAGE_SIZE - t_dec), F32)], axis=-1)
    cache_kr_t = jnp.swapaxes(cache_k_rope, 1, 2)
    cache_inv_t = jnp.swapaxes(cache_k_inv_rms, 1, 2)
    o_lat = _decode_attn(page_table, qa, qr, ckv_s, krope_s, g_new, cache_kv_latent, cache_kr_t, cache_inv_t,
                         pages_per_chunk=16)
    attn_s = _uv_proj(o_lat.reshape(nseq, N_HEADS, t_dec, KV_RANK), w_uv_h)
    to1 = lambda a: a.reshape(1, m_s, -1)
    h_s, n2_s = _merge(True, to1(xs2), to1(attn_s), to1(b_s), to1(u_s), state_conv.astype(F32), to1(ga_s),
                       to1(gc_s), w_conv, w_o_b, row(g_ffn), bm=256)
    y_sample = _ffn(n2_s.reshape(-1, d), h_s.reshape(-1, d), w_gu_b, w_down_b, bm=1024, bf=512).reshape(nseq, t_dec, d)
    u_s3 = u_s.reshape(nseq, t_dec, width)
    conv_state_sample = u_s3[:, t_dec - (CONV_K - 1):, :]

    return (y_prompt, y_sample, ckv_p, krope_p, kinv_p, conv_state_prompt,
            ckv_s, krope_s, kinv_s, conv_state_sample)
```

```python
import functools

import jax
import jax.numpy as jnp
from jax import lax
from jax.experimental import pallas as pl
from jax.experimental.pallas import tpu as pltpu

N_HEADS = 16
Q_RANK = 512
KV_RANK = 512
NOPE_DIM = 128
ROPE_DIM = 64
HALF_ROPE = ROPE_DIM // 2
V_DIM = 128
QK_DIM = NOPE_DIM + ROPE_DIM
QK_PAD = 256
ROPE_THETA = 10000.0
ATTN_SCALE = QK_DIM ** -0.5
LOG2_E = 1.4426950408889634
PAGE_SIZE = 128
RAW_SLOTS = 3
CONV_K = 3
EPS = 1e-6
NEG_INF = -1e30
LANES = 128
SUBLANES = 8
VMEM_LIMIT = 56 * 1024 * 1024

F32 = jnp.float32
BF16 = jnp.bfloat16
NT_DIMS = (((1,), (1,)), ((), ()))


def _params(*sem):
    return pltpu.CompilerParams(dimension_semantics=sem, vmem_limit_bytes=VMEM_LIMIT)


def _unit_rms(xf):
    return xf * lax.rsqrt(jnp.mean(xf * xf, axis=-1, keepdims=True) + EPS)


def _dot(a, b):
    return jnp.dot(a, b, preferred_element_type=F32)


def _dot_nt(a, b):
    return lax.dot_general(a, b, NT_DIMS, preferred_element_type=F32)


def _rest_kernel(x_ref, g_ref, wb_ref, wc_ref, wx_ref, wga_ref, wgc_ref, bga_ref, bgc_ref,
                 b_out, u_out, ga_out, gc_out, n_scr):
    @pl.when(pl.program_id(1) == 0)
    def _():
        n_scr[...] = (_unit_rms(x_ref[...]) * g_ref[...]).astype(BF16)

    n = n_scr[...]
    d_b = _dot(n, wb_ref[...])
    d_c = _dot(n, wc_ref[...])
    b_out[...] = d_b.astype(BF16)
    d_x = _dot(n, wx_ref[...])
    d_ga = _dot(n, wga_ref[...])
    u_out[...] = d_c * d_x
    d_gc = _dot(n, wgc_ref[...])
    ga_out[...] = jax.nn.sigmoid(d_ga + bga_ref[...]).astype(BF16)
    gc_out[...] = jax.nn.sigmoid(d_gc + bgc_ref[...]).astype(BF16)


def _rest_proj(x2, g_attn, w_b, w_c, w_x, w_ga, w_gc, b_ga, b_gc, *, bm, bn):
    m, d = x2.shape
    n = w_b.shape[1]
    row = lambda i, j: (i, 0)
    col = lambda i, j: (0, j)
    tile = lambda i, j: (i, j)
    w_spec = pl.BlockSpec((d, bn), col)
    v_spec = pl.BlockSpec((1, bn), col)
    o_spec = pl.BlockSpec((bm, bn), tile)
    return pl.pallas_call(
        _rest_kernel,
        grid=(m // bm, n // bn),
        in_specs=[pl.BlockSpec((bm, d), row), pl.BlockSpec((1, d), lambda i, j: (0, 0)),
                  w_spec, w_spec, w_spec, w_spec, w_spec, v_spec, v_spec],
        out_specs=[o_spec, o_spec, o_spec, o_spec],
        out_shape=[jax.ShapeDtypeStruct((m, n), BF16), jax.ShapeDtypeStruct((m, n), F32),
                   jax.ShapeDtypeStruct((m, n), BF16), jax.ShapeDtypeStruct((m, n), BF16)],
        scratch_shapes=[pltpu.VMEM((bm, d), BF16)],
        compiler_params=_params("arbitrary", "arbitrary"),
    )(x2, g_attn, w_b, w_c, w_x, w_ga, w_gc, b_ga, b_gc)


def _latent_kernel(prompt, x_ref, cos_ref, sin_ref, g_attn, g_q_lat, g_kv_lat, g_kr, g_q_nope, g_qr_a,
                   g_qr_b, g_k_nope, w_lat, w_uq, w_uk, w_aux, *outs):
    if prompt:
        qcat_out, kcat_out, vt_out, ckv_out, krope_out, kinv_out = outs
    else:
        qa_out, qr_out, ckv_out, krope_out, kinv_out = outs
    bm = x_ref.shape[1]
    cos_t = cos_ref[...]
    sin_t = sin_ref[...]
    lane = lax.broadcasted_iota(jnp.int32, (bm, LANES), 1)
    low_half = lane < ROPE_DIM
    q_scale = ATTN_SCALE * LOG2_E if prompt else ATTN_SCALE

    n = (_unit_rms(x_ref[0]) * g_attn[...]).astype(BF16)
    lat = _dot(n, w_lat[...])
    cq = (_unit_rms(lat[:, :Q_RANK]) * g_q_lat[...]).astype(BF16)
    ckv = _unit_rms(lat[:, Q_RANK:Q_RANK + KV_RANK]) * g_kv_lat[...]
    ckv_out[0] = ckv
    ckv_b = ckv.astype(BF16)

    krp = lat[:, Q_RANK + KV_RANK:]
    kr_ms = jnp.sum(jnp.where(low_half, krp * krp, 0.0), axis=-1, keepdims=True) * (1.0 / ROPE_DIM)
    krp = krp * lax.rsqrt(kr_ms + EPS) * g_kr[...]
    krt = krp * jnp.where(low_half, cos_t, sin_t)
    kr2 = krt + pltpu.roll(krt, ROPE_DIM, axis=1)
    krope_out[0] = kr2[:, :ROPE_DIM]

    kraw = _dot(ckv_b, w_uk[...])
    kinv = jnp.zeros((bm, LANES), F32)
    for h in range(N_HEADS):
        kh = kraw[:, h * NOPE_DIM:(h + 1) * NOPE_DIM]
        inv_h = lax.rsqrt(jnp.mean(kh * kh, axis=-1, keepdims=True) + EPS)
        kinv = jnp.where(lane == h, inv_h, kinv)
        if prompt:
            kcat_out[0, h, :, :NOPE_DIM] = (kh * inv_h * g_k_nope[...]).astype(BF16)
            kcat_out[0, h, :, NOPE_DIM:] = kr2.astype(BF16)
    kinv_out[0] = kinv[:, :N_HEADS]

    if prompt:
        vt = _dot_nt(w_aux[...], ckv_b)
        vt_out[0, :, 0] = vt.reshape(N_HEADS, V_DIM, bm).astype(BF16)

    q = _dot(cq, w_uq[...])
    rope_a0 = N_HEADS * NOPE_DIM
    rope_b0 = rope_a0 + N_HEADS * ROPE_DIM
    for h in range(N_HEADS):
        qh = _unit_rms(q[:, h * NOPE_DIM:(h + 1) * NOPE_DIM]) * (g_q_nope[...] * q_scale)
        if prompt:
            qcat_out[0, h, :, :NOPE_DIM] = qh.astype(BF16)
        else:
            qa_out[0, h] = _dot((qh * g_k_nope[...]).astype(BF16), w_aux[h])
    for p in range(N_HEADS // 2):
        qa_ = q[:, rope_a0 + p * LANES: rope_a0 + (p + 1) * LANES]
        qb_ = q[:, rope_b0 + p * LANES: rope_b0 + (p + 1) * LANES]
        sq = qa_ * qa_
        ms_lo = jnp.sum(jnp.where(low_half, sq, 0.0), axis=-1, keepdims=True) * (1.0 / ROPE_DIM)
        ms_hi = jnp.sum(jnp.where(low_half, 0.0, sq), axis=-1, keepdims=True) * (1.0 / ROPE_DIM)
        inv = jnp.where(low_half, lax.rsqrt(ms_lo + EPS), lax.rsqrt(ms_hi + EPS)) * q_scale
        rot = (qa_ * g_qr_a[...] * cos_t + qb_ * g_qr_b[...] * sin_t) * inv
        even = jnp.where(low_half, rot, 0.0)
        odd = jnp.where(low_half, 0.0, rot)
        if prompt:
            qcat_out[0, 2 * p, :, NOPE_DIM:] = even.astype(BF16)
            qcat_out[0, 2 * p + 1, :, NOPE_DIM:] = odd.astype(BF16)
        else:
            qr_out[0, 2 * p] = even
            qr_out[0, 2 * p + 1] = odd


def _latent_proj(prompt, x3, cos_t, sin_t, gains, w_lat, w_uq, w_uk, w_aux, *, bm):
    g, r, d = x3.shape
    nblk = r // bm
    tab_blocks = cos_t.shape[0] // bm
    const2 = lambda a: pl.BlockSpec(a.shape, lambda b, i: (0, 0), pipeline_mode=pl.Buffered(1))
    const3 = lambda a: pl.BlockSpec(a.shape, lambda b, i: (0, 0, 0), pipeline_mode=pl.Buffered(1))
    tab_spec = pl.BlockSpec((bm, LANES), lambda b, i: (i % tab_blocks, 0))
    row3 = lambda w: pl.BlockSpec((1, bm, w), lambda b, i: (b, i, 0))
    head4 = lambda w: pl.BlockSpec((1, N_HEADS, bm, w), lambda b, i: (b, 0, i, 0))
    if prompt:
        out_specs = [head4(QK_PAD), head4(QK_PAD),
                     pl.BlockSpec((1, N_HEADS, 1, V_DIM, bm), lambda b, i: (b, 0, i, 0, 0)),
                     row3(KV_RANK), row3(ROPE_DIM), row3(N_HEADS)]
        out_shape = [jax.ShapeDtypeStruct((g, N_HEADS, r, QK_PAD), BF16),
                     jax.ShapeDtypeStruct((g, N_HEADS, r, QK_PAD), BF16),
                     jax.ShapeDtypeStruct((g, N_HEADS, nblk, V_DIM, bm), BF16)]
    else:
        out_specs = [head4(KV_RANK), head4(LANES), row3(KV_RANK), row3(ROPE_DIM), row3(N_HEADS)]
        out_shape = [jax.ShapeDtypeStruct((g, N_HEADS, r, KV_RANK), F32),
                     jax.ShapeDtypeStruct((g, N_HEADS, r, LANES), F32)]
    out_shape += [jax.ShapeDtypeStruct((g, r, KV_RANK), F32), jax.ShapeDtypeStruct((g, r, ROPE_DIM), F32),
                  jax.ShapeDtypeStruct((g, r, N_HEADS), F32)]
    aux_spec = const2(w_aux) if prompt else const3(w_aux)
    return pl.pallas_call(
        functools.partial(_latent_kernel, prompt),
        grid=(g, nblk),
        in_specs=[pl.BlockSpec((1, bm, d), lambda b, i: (b, i, 0)), tab_spec, tab_spec]
                 + [const2(a) for a in gains] + [const2(w_lat), const2(w_uq), const2(w_uk), aux_spec],
        out_specs=out_specs,
        out_shape=out_shape,
        compiler_params=_params("arbitrary", "arbitrary"),
    )(x3, cos_t, sin_t, *gains, w_lat, w_uq, w_uk, w_aux)


def _prompt_attn_kernel(q_ref, k_ref, vt_ref, o_ref, st0, st1, cm0, cm1, m_scr, l_scr, acc_scr, *, tq, vb, hps):
    qi = pl.program_id(2)
    sts, cms = (st0, st1), (cm0, cm1)
    nv = tq // vb
    m_scr[...] = jnp.full(m_scr.shape, NEG_INF, F32)
    l_scr[...] = jnp.zeros(l_scr.shape, F32)
    acc_scr[...] = jnp.zeros(acc_scr.shape, F32)

    def q_phase(jk, slot, masked):
        for hh in range(hps):
            k = k_ref[0, hh, pl.ds(pl.multiple_of(jk * tq, tq), tq), :]
            st = _dot_nt(k, q_ref[0, hh])
            if masked:
                kpos = lax.broadcasted_iota(jnp.int32, (tq, tq), 0)
                qpos = lax.broadcasted_iota(jnp.int32, (tq, tq), 1)
                st = jnp.where(kpos <= qpos, st, NEG_INF)
            sts[slot][hh] = st
            cms[slot][hh] = jnp.max(st, axis=0, keepdims=True)

    def sp_phase(jv, slot):
        for hh in range(hps):
            m_prev = m_scr[hh]
            m_new = jnp.maximum(m_prev, cms[slot][hh])
            alpha = jnp.exp2(m_prev - m_new)
            p = jnp.exp2(sts[slot][hh] - m_new)
            l_scr[hh] = alpha * l_scr[hh] + jnp.sum(p, axis=0, keepdims=True)
            pb = p.astype(BF16)
            pv = _dot(vt_ref[0, hh, jv * nv], pb[:vb])
            for s in range(1, nv):
                pv += _dot(vt_ref[0, hh, jv * nv + s], pb[s * vb:(s + 1) * vb])
            acc_scr[hh] = alpha * acc_scr[hh] + pv
            m_scr[hh] = m_new

    q_phase(qi, 0, True)

    def two_visits(i, carry):
        t = 2 * i
        q_phase(t, 1, False)
        sp_phase(jnp.where(t == 0, qi, t - 1), 0)
        q_phase(t + 1, 0, False)
        sp_phase(t, 1)
        return carry

    lax.fori_loop(0, qi // 2, two_visits, 0)

    @pl.when(qi % 2 == 1)
    def _():
        t = qi - 1
        q_phase(t, 1, False)
        sp_phase(jnp.where(t == 0, qi, t - 1), 0)
        sp_phase(t, 1)

    @pl.when(qi % 2 == 0)
    def _():
        sp_phase(jnp.where(qi == 0, qi, qi - 1), 0)

    for hh in range(hps):
        o = acc_scr[hh] * (1.0 / l_scr[hh])
        o_ref[0, :, hh * V_DIM:(hh + 1) * V_DIM] = o.T.astype(BF16)


def _prompt_attn(qcat, kcat, vt, *, tq, heads_per_step):
    b, h, s, w = qcat.shape
    nvb, vb = vt.shape[2], vt.shape[4]
    hps = heads_per_step
    return pl.pallas_call(
        functools.partial(_prompt_attn_kernel, tq=tq, vb=vb, hps=hps),
        grid=(b, h // hps, s // tq),
        in_specs=[pl.BlockSpec((1, hps, tq, w), lambda bi, hi, qi: (bi, hi, qi, 0)),
                  pl.BlockSpec((1, hps, s, w), lambda bi, hi, qi: (bi, hi, 0, 0)),
                  pl.BlockSpec((1, hps, nvb, V_DIM, vb), lambda bi, hi, qi: (bi, hi, 0, 0, 0))],
        out_specs=pl.BlockSpec((1, tq, hps * V_DIM), lambda bi, hi, qi: (bi, qi, hi)),
        out_shape=jax.ShapeDtypeStruct((b, s, h * V_DIM), BF16),
        scratch_shapes=[pltpu.VMEM((hps, tq, tq), F32), pltpu.VMEM((hps, tq, tq), F32),
                        pltpu.VMEM((hps, 1, tq), F32), pltpu.VMEM((hps, 1, tq), F32),
                        pltpu.VMEM((hps, 1, tq), F32), pltpu.VMEM((hps, 1, tq), F32),
                        pltpu.VMEM((hps, V_DIM, tq), F32)],
        compiler_params=_params("arbitrary", "arbitrary", "arbitrary"),
    )(qcat, kcat, vt)


def _expand_heads(inv_t):
    k = inv_t.shape[1]
    return jnp.broadcast_to(inv_t[:, None, :], (N_HEADS, SUBLANES, k)).reshape(N_HEADS * SUBLANES, k)


def _decode_kernel(nchunk, gpages, pt_ref, qa_ref, qr_ref, cnew_ref, krnew_ref, gnew_ref,
                   cache_c, cache_kr, cache_inv, o_ref,
                   raw_c, raw_kr, raw_inv, cb0, cb1, s0, s1,
                   qa_scr, qr_scr, m_scr, l_scr, acc_scr, lsum_scr, sems):
    b = pl.program_id(0)
    nseq = pl.num_programs(0)
    rows = N_HEADS * SUBLANES
    cbs, ss = (cb0, cb1), (s0, s1)
    raw_bufs = (raw_c, raw_kr, raw_inv)
    caches = (cache_c, cache_kr, cache_inv)

    def start_chunk(seq, chunk, r):
        for p in range(gpages):
            page = pt_ref[seq, chunk * gpages + p]
            for a in range(3):
                pltpu.make_async_copy(caches[a].at[page], raw_bufs[a].at[r, p], sems.at[r, a]).start()

    def wait_chunk(r):
        for a in range(3):
            pltpu.make_async_copy(caches[a].at[pl.ds(0, gpages)], raw_bufs[a].at[r], sems.at[r, a]).wait()

    @pl.when(b == 0)
    def _():
        for g in range(RAW_SLOTS):
            start_chunk(0, g, g)

    qa_scr[...] = qa_ref[0].reshape(rows, KV_RANK).astype(BF16)
    qr_in = qr_ref[0].reshape(rows, LANES)
    qr_scr[...] = (qr_in[:, :ROPE_DIM] + qr_in[:, ROPE_DIM:]).astype(BF16)
    m_scr[...] = jnp.full(m_scr.shape, NEG_INF, F32)
    l_scr[...] = jnp.zeros(l_scr.shape, F32)
    acc_scr[...] = jnp.zeros(acc_scr.shape, F32)

    unit = 2 * PAGE_SIZE
    nunit = gpages // 2

    def a_unit(r, slot, u):
        p0, p1 = 2 * u, 2 * u + 1
        cb = jnp.concatenate([raw_c[r, p0], raw_c[r, p1]], axis=0).astype(BF16)
        cbs[slot][u * unit:(u + 1) * unit, :] = cb
        inv2 = jnp.concatenate([raw_inv[r, p0], raw_inv[r, p1]], axis=1)
        kr2 = jnp.concatenate([raw_kr[r, p0], raw_kr[r, p1]], axis=1).astype(BF16)
        s = _dot_nt(qa_scr[...], cb) * _expand_heads(inv2) + _dot(qr_scr[...], kr2)
        ss[slot][:, u * unit:(u + 1) * unit] = s

    def b_prelude(slot):
        m_prev = m_scr[...]
        m_new = jnp.maximum(m_prev, jnp.max(ss[slot][...], axis=-1, keepdims=True))
        alpha = jnp.exp(m_prev - m_new)
        acc_scr[...] = alpha * acc_scr[...]
        l_scr[...] = alpha * l_scr[...]
        lsum_scr[...] = jnp.zeros(lsum_scr.shape, F32)
        m_scr[...] = m_new

    def b_unit(slot, u):
        p = jnp.exp(ss[slot][:, u * unit:(u + 1) * unit] - m_scr[...])
        lsum_scr[...] += p[:, :LANES] + p[:, LANES:]
        acc_scr[...] += _dot(p.astype(BF16), cbs[slot][u * unit:(u + 1) * unit, :])

    def b_finish():
        l_scr[...] += jnp.sum(lsum_scr[...], axis=-1, keepdims=True)

    def refill(chunk, r):
        nxt = chunk + RAW_SLOTS
        wraps = nxt >= nchunk
        seq2 = jnp.minimum(jnp.where(wraps, b + 1, b), nseq - 1)
        start_chunk(seq2, jnp.where(wraps, nxt - nchunk, nxt), r)

    def stage(chunk_a, slot_a, slot_b):
        if slot_a is not None:
            r = lax.rem(b * nchunk + chunk_a, RAW_SLOTS)
            wait_chunk(r)
        for u in range(nunit):
            if slot_a is not None:
                a_unit(r, slot_a, u)
            if slot_b is not None:
                if u == 0:
                    b_prelude(slot_b)
                b_unit(slot_b, u)
        if slot_b is not None:
            b_finish()
        if slot_a is not None:
            refill(chunk_a, r)

    stage(0, 0, None)

    def pair(cc, carry):
        stage(2 * cc + 1, 1, 0)
        stage(2 * cc + 2, 0, 1)
        return carry

    lax.fori_loop(0, (nchunk - 2) // 2, pair, 0)
    stage(nchunk - 1, 1, 0)
    stage(None, None, 1)

    @pl.when(b == nseq - 1)
    def _():
        for r in range(RAW_SLOTS):
            wait_chunk(r)

    def update(s, values_b):
        m_prev = m_scr[...]
        m_new = jnp.maximum(m_prev, jnp.max(s, axis=-1, keepdims=True))
        p = jnp.exp(s - m_new)
        alpha = jnp.exp(m_prev - m_new)
        l_scr[...] = alpha * l_scr[...] + jnp.sum(p, axis=-1, keepdims=True)
        acc_scr[...] = alpha * acc_scr[...] + _dot(p.astype(BF16), values_b)
        m_scr[...] = m_new

    t_new = cnew_ref.shape[1]
    pad = lambda a: jnp.concatenate([a, jnp.zeros((PAGE_SIZE - t_new, a.shape[1]), F32)], axis=0).astype(BF16)
    cn = pad(cnew_ref[0])
    s_new = _dot_nt(qa_scr[...], cn) * gnew_ref[0] + _dot_nt(qr_scr[...], pad(krnew_ref[0]))
    tok = lax.broadcasted_iota(jnp.int32, (rows, PAGE_SIZE), 0) % SUBLANES
    key = lax.broadcasted_iota(jnp.int32, (rows, PAGE_SIZE), 1)
    update(jnp.where(key <= tok, s_new, NEG_INF), cn)
    o_ref[0] = acc_scr[...] * (1.0 / l_scr[...])


def _decode_attn(page_table, qa, qr, c_new, kr_new, g_new, cache_c, cache_kr, cache_inv, *, pages_per_chunk):
    nseq, npages = page_table.shape
    t_new = c_new.shape[1]
    gpages = pages_per_chunk
    nchunk = npages // gpages
    assert t_new == SUBLANES and npages % gpages == 0 and nchunk % 2 == 0 and nchunk > RAW_SLOTS
    rows = N_HEADS * t_new
    keys = gpages * PAGE_SIZE
    any_spec = pl.BlockSpec(memory_space=pl.ANY)
    in_specs = [pl.BlockSpec((1, N_HEADS, t_new, KV_RANK), lambda b, pt: (0, 0, b, 0)),
                pl.BlockSpec((1, N_HEADS, t_new, LANES), lambda b, pt: (0, 0, b, 0)),
                pl.BlockSpec((1, t_new, KV_RANK), lambda b, pt: (b, 0, 0)),
                pl.BlockSpec((1, t_new, ROPE_DIM), lambda b, pt: (b, 0, 0)),
                pl.BlockSpec((1, rows, PAGE_SIZE), lambda b, pt: (b, 0, 0)),
                any_spec, any_spec, any_spec]
    two = lambda shape, dtype: [pltpu.VMEM(shape, dtype), pltpu.VMEM(shape, dtype)]
    grid_spec = pltpu.PrefetchScalarGridSpec(
        num_scalar_prefetch=1,
        grid=(nseq,),
        in_specs=in_specs,
        out_specs=pl.BlockSpec((1, rows, KV_RANK), lambda b, pt: (b, 0, 0)),
        scratch_shapes=[pltpu.VMEM((RAW_SLOTS, gpages, PAGE_SIZE, KV_RANK), F32),
                        pltpu.VMEM((RAW_SLOTS, gpages, ROPE_DIM, PAGE_SIZE), F32),
                        pltpu.VMEM((RAW_SLOTS, gpages, N_HEADS, PAGE_SIZE), F32)]
                       + two((keys, KV_RANK), BF16) + two((rows, keys), F32)
                       + [pltpu.VMEM((rows, KV_RANK), BF16), pltpu.VMEM((rows, ROPE_DIM), BF16),
                          pltpu.VMEM((rows, 1), F32), pltpu.VMEM((rows, 1), F32), pltpu.VMEM((rows, KV_RANK), F32),
                          pltpu.VMEM((rows, LANES), F32), pltpu.SemaphoreType.DMA((RAW_SLOTS, 3))],
    )
    return pl.pallas_call(
        functools.partial(_decode_kernel, nchunk, gpages),
        grid_spec=grid_spec,
        out_shape=jax.ShapeDtypeStruct((nseq, rows, KV_RANK), F32),
        compiler_params=_params("arbitrary"),
    )(page_table, qa, qr, c_new, kr_new, g_new, cache_c, cache_kr, cache_inv)


def _uv_kernel(o_ref, w_ref, out_ref):
    nseq, _, t, r = o_ref.shape
    o = o_ref[...].reshape(nseq * t, r).astype(BF16)
    out_ref[...] = _dot(o, w_ref[0]).astype(BF16)


def _uv_proj(o_lat4, w_uv_h):
    nseq, h, t, r = o_lat4.shape
    return pl.pallas_call(
        _uv_kernel,
        grid=(h,),
        in_specs=[pl.BlockSpec((nseq, 1, t, r), lambda i: (0, i, 0, 0)),
                  pl.BlockSpec((1, r, V_DIM), lambda i: (i, 0, 0))],
        out_specs=pl.BlockSpec((nseq * t, V_DIM), lambda i: (0, i)),
        out_shape=jax.ShapeDtypeStruct((nseq * t, h * V_DIM), BF16),
        compiler_params=_params("arbitrary"),
    )(o_lat4, w_uv_h)


def _merge_kernel(short_seq, x_ref, attn_ref, b_ref, u_ref, prev_ref, ga_ref, gc_ref, wconv_ref, wo_ref,
                  gffn_ref, h_out, n2_out):
    u = u_ref[...]
    bm, c = u.shape
    if short_seq:
        nseq = bm // SUBLANES
        u3 = u.reshape(nseq, SUBLANES, c)
        tok = lax.broadcasted_iota(jnp.int32, (nseq, SUBLANES, c), 1)
        st = prev_ref[...]
        s0, s1 = st[:, 0:1, :], st[:, 1:2, :]
        u1 = jnp.where(tok == 0, s1, pltpu.roll(u3, 1, axis=1))
        u2 = jnp.where(tok == 0, s0, jnp.where(tok == 1, s1, pltpu.roll(u3, 2, axis=1)))
        u1 = u1.reshape(bm, c)
        u2 = u2.reshape(bm, c)
    else:
        first = pl.program_id(1) == 0
        halo = jnp.where(first, 0.0, prev_ref[0])
        row = lax.broadcasted_iota(jnp.int32, (bm, c), 0)
        h1, h2 = halo[SUBLANES - 1:SUBLANES, :], halo[SUBLANES - 2:SUBLANES - 1, :]
        u1 = jnp.where(row == 0, h1, pltpu.roll(u, 1, axis=0))
        u2 = jnp.where(row == 0, h2, jnp.where(row == 1, h1, pltpu.roll(u, 2, axis=0)))
    wc = wconv_ref[...]
    conv = wc[0:1, :] * u2 + wc[1:2, :] * u1 + wc[2:3, :] * u
    mixed = ga_ref[...].astype(F32) * attn_ref[...].astype(F32) + gc_ref[...].astype(F32) * (b_ref[...].astype(F32) * conv)
    h = x_ref[...] + _dot(mixed.astype(BF16), wo_ref[...])
    h_out[...] = h
    n2_out[...] = (_unit_rms(h) * gffn_ref[...]).astype(BF16)


def _merge(short_seq, x3, attn3, b3, u3, prev, ga3, gc3, w_conv, w_o, g_ffn, *, bm):
    g, r, d = x3.shape
    row = pl.BlockSpec((None, bm, d), lambda b, i: (b, i, 0))
    if short_seq:
        prev_spec = pl.BlockSpec((bm // SUBLANES, CONV_K - 1, d), lambda b, i: (i, 0, 0))
    else:
        per = bm // SUBLANES
        prev_spec = pl.BlockSpec((1, SUBLANES, d), lambda b, i: (b, jnp.maximum(i * per - 1, 0), 0))
    return pl.pallas_call(
        functools.partial(_merge_kernel, short_seq),
        grid=(g, r // bm),
        in_specs=[row, row, row, row, prev_spec, row, row,
                  pl.BlockSpec(w_conv.shape, lambda b, i: (0, 0)),
                  pl.BlockSpec(w_o.shape, lambda b, i: (0, 0), pipeline_mode=pl.Buffered(1)),
                  pl.BlockSpec(g_ffn.shape, lambda b, i: (0, 0))],
        out_specs=[row, row],
        out_shape=[jax.ShapeDtypeStruct((g, r, d), F32), jax.ShapeDtypeStruct((g, r, d), BF16)],
        compiler_params=_params("arbitrary", "arbitrary"),
    )(x3, attn3, b3, u3, prev, ga3, gc3, w_conv, w_o, g_ffn)


def _ffn_kernel(n2_ref, h_ref, wg_ref, wu_ref, wd_ref, y_ref):
    @pl.when(pl.program_id(1) == 0)
    def _():
        y_ref[...] = h_ref[...]

    n2 = n2_ref[...]
    act = (jax.nn.silu(_dot(n2, wg_ref[...])) * _dot(n2, wu_ref[...])).astype(BF16)
    y_ref[...] += _dot(act, wd_ref[...])


def _ffn(n2, h, w_gate_up, w_down, *, bm, bf):
    m, d = h.shape
    d_ff = w_down.shape[0]
    nf = d_ff // bf
    row = pl.BlockSpec((bm, d), lambda i, f: (i, 0))
    return pl.pallas_call(
        _ffn_kernel,
        grid=(m // bm, nf),
        in_specs=[row,
                  pl.BlockSpec((bm, d), lambda i, f: (i, 0), pipeline_mode=pl.Buffered(1)),
                  pl.BlockSpec((d, bf), lambda i, f: (0, f)),
                  pl.BlockSpec((d, bf), lambda i, f: (0, f + nf)),
                  pl.BlockSpec((bf, d), lambda i, f: (f, 0))],
        out_specs=row,
        out_shape=jax.ShapeDtypeStruct((m, d), F32),
        compiler_params=_params("arbitrary", "arbitrary"),
    )(n2, h, w_gate_up, w_gate_up, w_down)


def _rope_tables(pos):
    freqs = ROPE_THETA ** (-jnp.arange(HALF_ROPE, dtype=F32) / HALF_ROPE)
    ang = pos.astype(F32)[:, None] * freqs
    c, s = jnp.cos(ang), jnp.sin(ang)
    return jnp.concatenate([c, c, c, c], axis=-1), jnp.concatenate([-s, s, -s, s], axis=-1)


def _swap_halves(a):
    return jnp.concatenate([a[..., HALF_ROPE:], a[..., :HALF_ROPE]], axis=-1)


def kernel(x_prompt, x_sample, cache_kv_latent, cache_k_rope, cache_k_inv_rms, state_conv, page_table,
           g_attn, w_in, b_gate, g_q_lat, g_kv_lat, w_uq, g_q_nope, g_q_rope, g_k_nope, g_k_rope,
           w_uk, w_uv, w_conv, w_o, g_ffn, w_gate_up, w_down):
    batch, seq, d = x_prompt.shape
    nseq, t_dec, _ = x_sample.shape
    past_len = page_table.shape[1] * PAGE_SIZE
    width = w_conv.shape[1]
    d_ff = w_down.shape[0]

    i1 = Q_RANK + KV_RANK
    i2 = i1 + ROPE_DIM
    i3, i4, i5 = i2 + width, i2 + 2 * width, i2 + 3 * width
    w_lat = jnp.concatenate([w_in[:, :i2], _swap_halves(w_in[:, i1:i2])], axis=1).astype(BF16)
    w_b, w_c, w_x = (w_in[:, a:a + width].astype(BF16) for a in (i2, i3, i4))
    w_ga, w_gc = w_in[:, i5:i5 + d].astype(BF16), w_in[:, i5 + d:].astype(BF16)
    b_ga, b_gc = b_gate[:d].reshape(1, d), b_gate[d:].reshape(1, d)
    uq_rope = w_uq[:, :, NOPE_DIM:]
    w_uq_all = jnp.concatenate([w_uq[:, :, :NOPE_DIM].reshape(Q_RANK, -1), uq_rope.reshape(Q_RANK, -1),
                                _swap_halves(uq_rope).reshape(Q_RANK, -1)], axis=1).astype(BF16)
    w_uk2 = w_uk.reshape(KV_RANK, -1).astype(BF16)
    w_uk_t = jnp.transpose(w_uk, (1, 2, 0)).astype(BF16)
    w_uv_t = jnp.transpose(w_uv, (1, 2, 0)).reshape(N_HEADS * V_DIM, KV_RANK).astype(BF16)
    w_uv_h = jnp.transpose(w_uv, (1, 0, 2)).astype(BF16)
    w_o_b, w_gu_b, w_down_b = w_o.astype(BF16), w_gate_up.astype(BF16), w_down.astype(BF16)
    row = lambda a: a.reshape(1, -1)
    g_kr = row(jnp.concatenate([g_k_rope, _swap_halves(g_k_rope)]))
    g_qr_a = row(jnp.concatenate([g_q_rope, g_q_rope]))
    g_qr_b = row(jnp.concatenate([_swap_halves(g_q_rope)] * 2))
    gains = [row(g_attn), row(g_q_lat), row(g_kv_lat), g_kr, row(g_q_nope), g_qr_a, g_qr_b, row(g_k_nope)]

    bm_lat = 256
    cos_p, sin_p = _rope_tables(jnp.arange(seq))
    cos_s, sin_s = _rope_tables(jnp.tile(past_len + jnp.arange(t_dec), bm_lat // t_dec))

    xp2 = x_prompt.reshape(batch * seq, d)
    b_p, u_p, ga_p, gc_p = _rest_proj(xp2, row(g_attn), w_b, w_c, w_x, w_ga, w_gc, b_ga, b_gc, bm=1024, bn=256)
    qcat, kcat, vt, ckv_p, krope_p, kinv_p = _latent_proj(
        True, x_prompt, cos_p, sin_p, gains, w_lat, w_uq_all, w_uk2, w_uv_t, bm=bm_lat)
    attn_p = _prompt_attn(qcat, kcat, vt, tq=512, heads_per_step=2)
    to3 = lambda a: a.reshape(batch, seq, -1)
    u_p3 = to3(u_p)
    h_p, n2_p = _merge(False, x_prompt, attn_p, to3(b_p), u_p3, u_p3, to3(ga_p), to3(gc_p), w_conv, w_o_b,
                       row(g_ffn), bm=256)
    y_prompt = _ffn(n2_p.reshape(-1, d), h_p.reshape(-1, d), w_gu_b, w_down_b, bm=1024, bf=512).reshape(batch, seq, d)
    conv_state_prompt = u_p3[:, seq - (CONV_K - 1):, :]

    m_s = nseq * t_dec
    xs2 = x_sample.reshape(m_s, d)
    b_s, u_s, ga_s, gc_s = _rest_proj(xs2, row(g_attn), w_b, w_c, w_x, w_ga, w_gc, b_ga, b_gc, bm=1024, bn=256)
    qa, qr, ckv_s, krope_s, kinv_s = _latent_proj(
        False, xs2.reshape(1, m_s, d), cos_s, sin_s, gains, w_lat, w_uq_all, w_uk2, w_uk_t, bm=bm_lat)
    ckv_s = ckv_s.reshape(nseq, t_dec, KV_RANK)
    krope_s = krope_s.reshape(nseq, t_dec, ROPE_DIM)
    kinv_s = kinv_s.reshape(nseq, t_dec, N_HEADS)
    g_new = jnp.repeat(jnp.swapaxes(kinv_s, 1, 2), t_dec, axis=1)
    g_new = jnp.concatenate([g_new, jnp.ones((nseq, N_HEADS * t_dec, PAGE_SIZE - t_dec), F32)], axis=-1)
    cache_kr_t = jnp.swapaxes(cache_k_rope, 1, 2)
    cache_inv_t = jnp.swapaxes(cache_k_inv_rms, 1, 2)
    o_lat = _decode_attn(page_table, qa, qr, ckv_s, krope_s, g_new, cache_kv_latent, cache_kr_t, cache_inv_t,
                         pages_per_chunk=16)
    attn_s = _uv_proj(o_lat.reshape(nseq, N_HEADS, t_dec, KV_RANK), w_uv_h)
    to1 = lambda a: a.reshape(1, m_s, -1)
    h_s, n2_s = _merge(True, to1(xs2), to1(attn_s), to1(b_s), to1(u_s), state_conv.astype(F32), to1(ga_s),
                       to1(gc_s), w_conv, w_o_b, row(g_ffn), bm=256)
    y_sample = _ffn(n2_s.reshape(-1, d), h_s.reshape(-1, d), w_gu_b, w_down_b, bm=1024, bf=512).reshape(nseq, t_dec, d)
    u_s3 = u_s.reshape(nseq, t_dec, width)
    conv_state_sample = u_s3[:, t_dec - (CONV_K - 1):, :]

    return (y_prompt, y_sample, ckv_p, krope_p, kinv_p, conv_state_prompt,
            ckv_s, krope_s, kinv_s, conv_state_sample)
```

```python
import functools

import jax
import jax.numpy as jnp
from jax import lax
from jax.experimental import pallas as pl
from jax.experimental.pallas import tpu as pltpu

N_HEADS = 16
Q_RANK = 512
KV_RANK = 512
NOPE_DIM = 128
ROPE_DIM = 64
HALF_ROPE = ROPE_DIM // 2
V_DIM = 128
QK_DIM = NOPE_DIM + ROPE_DIM
QK_PAD = 256
ROPE_THETA = 10000.0
ATTN_SCALE = QK_DIM ** -0.5
LOG2_E = 1.4426950408889634
PAGE_SIZE = 128
RAW_SLOTS = 3
CONV_K = 3
EPS = 1e-6
NEG_INF = -1e30
LANES = 128
SUBLANES = 8
VMEM_LIMIT = 56 * 1024 * 1024

F32 = jnp.float32
BF16 = jnp.bfloat16
NT_DIMS = (((1,), (1,)), ((), ()))


def _params(*sem):
    return pltpu.CompilerParams(dimension_semantics=sem, vmem_limit_bytes=VMEM_LIMIT)


def _unit_rms(xf):
    return xf * lax.rsqrt(jnp.mean(xf * xf, axis=-1, keepdims=True) + EPS)


def _dot(a, b):
    return jnp.dot(a, b, preferred_element_type=F32)


def _dot_nt(a, b):
    return lax.dot_general(a, b, NT_DIMS, preferred_element_type=F32)


def _rest_kernel(x_ref, g_ref, wb_ref, wc_ref, wx_ref, wga_ref, wgc_ref, bga_ref, bgc_ref,
                 b_out, u_out, ga_out, gc_out, n_scr):
    @pl.when(pl.program_id(1) == 0)
    def _():
        n_scr[...] = (_unit_rms(x_ref[...]) * g_ref[...]).astype(BF16)

    n = n_scr[...]
    d_b = _dot(n, wb_ref[...])
    d_c = _dot(n, wc_ref[...])
    b_out[...] = d_b.astype(BF16)
    d_x = _dot(n, wx_ref[...])
    d_ga = _dot(n, wga_ref[...])
    u_out[...] = d_c * d_x
    d_gc = _dot(n, wgc_ref[...])
    ga_out[...] = jax.nn.sigmoid(d_ga + bga_ref[...]).astype(BF16)
    gc_out[...] = jax.nn.sigmoid(d_gc + bgc_ref[...]).astype(BF16)


def _rest_proj(x2, g_attn, w_b, w_c, w_x, w_ga, w_gc, b_ga, b_gc, *, bm, bn):
    m, d = x2.shape
    n = w_b.shape[1]
    row = lambda i, j: (i, 0)
    col = lambda i, j: (0, j)
    tile = lambda i, j: (i, j)
    w_spec = pl.BlockSpec((d, bn), col)
    v_spec = pl.BlockSpec((1, bn), col)
    o_spec = pl.BlockSpec((bm, bn), tile)
    return pl.pallas_call(
        _rest_kernel,
        grid=(m // bm, n // bn),
        in_specs=[pl.BlockSpec((bm, d), row), pl.BlockSpec((1, d), lambda i, j: (0, 0)),
                  w_spec, w_spec, w_spec, w_spec, w_spec, v_spec, v_spec],
        out_specs=[o_spec, o_spec, o_spec, o_spec],
        out_shape=[jax.ShapeDtypeStruct((m, n), BF16), jax.ShapeDtypeStruct((m, n), F32),
                   jax.ShapeDtypeStruct((m, n), BF16), jax.ShapeDtypeStruct((m, n), BF16)],
        scratch_shapes=[pltpu.VMEM((bm, d), BF16)],
        compiler_params=_params("arbitrary", "arbitrary"),
    )(x2, g_attn, w_b, w_c, w_x, w_ga, w_gc, b_ga, b_gc)


def _latent_kernel(prompt, x_ref, cos_ref, sin_ref, g_attn, g_q_lat, g_kv_lat, g_kr, g_q_nope, g_qr_a,
                   g_qr_b, g_k_nope, w_lat, w_uq, w_uk, w_aux, *outs):
    if prompt:
        qcat_out, kcat_out, vt_out, ckv_out, krope_out, kinv_out = outs
    else:
        qa_out, qr_out, ckv_out, krope_out, kinv_out = outs
    bm = x_ref.shape[1]
    cos_t = cos_ref[...]
    sin_t = sin_ref[...]
    lane = lax.broadcasted_iota(jnp.int32, (bm, LANES), 1)
    low_half = lane < ROPE_DIM
    q_scale = ATTN_SCALE * LOG2_E if prompt else ATTN_SCALE

    n = (_unit_rms(x_ref[0]) * g_attn[...]).astype(BF16)
    lat = _dot(n, w_lat[...])
    cq = (_unit_rms(lat[:, :Q_RANK]) * g_q_lat[...]).astype(BF16)
    ckv = _unit_rms(lat[:, Q_RANK:Q_RANK + KV_RANK]) * g_kv_lat[...]
    ckv_out[0] = ckv
    ckv_b = ckv.astype(BF16)

    krp = lat[:, Q_RANK + KV_RANK:]
    kr_ms = jnp.sum(jnp.where(low_half, krp * krp, 0.0), axis=-1, keepdims=True) * (1.0 / ROPE_DIM)
    krp = krp * lax.rsqrt(kr_ms + EPS) * g_kr[...]
    krt = krp * jnp.where(low_half, cos_t, sin_t)
    kr2 = krt + pltpu.roll(krt, ROPE_DIM, axis=1)
    krope_out[0] = kr2[:, :ROPE_DIM]

    kraw = _dot(ckv_b, w_uk[...])
    kinv = jnp.zeros((bm, LANES), F32)
    for h in range(N_HEADS):
        kh = kraw[:, h * NOPE_DIM:(h + 1) * NOPE_DIM]
        inv_h = lax.rsqrt(jnp.mean(kh * kh, axis=-1, keepdims=True) + EPS)
        kinv = jnp.where(lane == h, inv_h, kinv)
        if prompt:
            kcat_out[0, h, :, :NOPE_DIM] = (kh * inv_h * g_k_nope[...]).astype(BF16)
            kcat_out[0, h, :, NOPE_DIM:] = kr2.astype(BF16)
    kinv_out[0] = kinv[:, :N_HEADS]

    if prompt:
        vt = _dot_nt(w_aux[...], ckv_b)
        vt_out[0, :, 0] = vt.reshape(N_HEADS, V_DIM, bm).astype(BF16)

    q = _dot(cq, w_uq[...])
    rope_a0 = N_HEADS * NOPE_DIM
    rope_b0 = rope_a0 + N_HEADS * ROPE_DIM
    for h in range(N_HEADS):
        qh = _unit_rms(q[:, h * NOPE_DIM:(h + 1) * NOPE_DIM]) * (g_q_nope[...] * q_scale)
        if prompt:
            qcat_out[0, h, :, :NOPE_DIM] = qh.astype(BF16)
        else:
            qa_out[0, h] = _dot((qh * g_k_nope[...]).astype(BF16), w_aux[h])
    for p in range(N_HEADS // 2):
        qa_ = q[:, rope_a0 + p * LANES: rope_a0 + (p + 1) * LANES]
        qb_ = q[:, rope_b0 + p * LANES: rope_b0 + (p + 1) * LANES]
        sq = qa_ * qa_
        ms_lo = jnp.sum(jnp.where(low_half, sq, 0.0), axis=-1, keepdims=True) * (1.0 / ROPE_DIM)
        ms_hi = jnp.sum(jnp.where(low_half, 0.0, sq), axis=-1, keepdims=True) * (1.0 / ROPE_DIM)
        inv = jnp.where(low_half, lax.rsqrt(ms_lo + EPS), lax.rsqrt(ms_hi + EPS)) * q_scale
        rot = (qa_ * g_qr_a[...] * cos_t + qb_ * g_qr_b[...] * sin_t) * inv
        even = jnp.where(low_half, rot, 0.0)
        odd = jnp.where(low_half, 0.0, rot)
        if prompt:
            qcat_out[0, 2 * p, :, NOPE_DIM:] = even.astype(BF16)
            qcat_out[0, 2 * p + 1, :, NOPE_DIM:] = odd.astype(BF16)
        else:
            qr_out[0, 2 * p] = even
            qr_out[0, 2 * p + 1] = odd


def _latent_proj(prompt, x3, cos_t, sin_t, gains, w_lat, w_uq, w_uk, w_aux, *, bm):
    g, r, d = x3.shape
    nblk = r // bm
    tab_blocks = cos_t.shape[0] // bm
    const2 = lambda a: pl.BlockSpec(a.shape, lambda b, i: (0, 0), pipeline_mode=pl.Buffered(1))
    const3 = lambda a: pl.BlockSpec(a.shape, lambda b, i: (0, 0, 0), pipeline_mode=pl.Buffered(1))
    tab_spec = pl.BlockSpec((bm, LANES), lambda b, i: (i % tab_blocks, 0))
    row3 = lambda w: pl.BlockSpec((1, bm, w), lambda b, i: (b, i, 0))
    head4 = lambda w: pl.BlockSpec((1, N_HEADS, bm, w), lambda b, i: (b, 0, i, 0))
    if prompt:
        out_specs = [head4(QK_PAD), head4(QK_PAD),
                     pl.BlockSpec((1, N_HEADS, 1, V_DIM, bm), lambda b, i: (b, 0, i, 0, 0)),
                     row3(KV_RANK), row3(ROPE_DIM), row3(N_HEADS)]
        out_shape = [jax.ShapeDtypeStruct((g, N_HEADS, r, QK_PAD), BF16),
                     jax.ShapeDtypeStruct((g, N_HEADS, r, QK_PAD), BF16),
                     jax.ShapeDtypeStruct((g, N_HEADS, nblk, V_DIM, bm), BF16)]
    else:
        out_specs = [head4(KV_RANK), head4(LANES), row3(KV_RANK), row3(ROPE_DIM), row3(N_HEADS)]
        out_shape = [jax.ShapeDtypeStruct((g, N_HEADS, r, KV_RANK), F32),
                     jax.ShapeDtypeStruct((g, N_HEADS, r, LANES), F32)]
    out_shape += [jax.ShapeDtypeStruct((g, r, KV_RANK), F32), jax.ShapeDtypeStruct((g, r, ROPE_DIM), F32),
                  jax.ShapeDtypeStruct((g, r, N_HEADS), F32)]
    aux_spec = const2(w_aux) if prompt else const3(w_aux)
    return pl.pallas_call(
        functools.partial(_latent_kernel, prompt),
        grid=(g, nblk),
        in_specs=[pl.BlockSpec((1, bm, d), lambda b, i: (b, i, 0)), tab_spec, tab_spec]
                 + [const2(a) for a in gains] + [const2(w_lat), const2(w_uq), const2(w_uk), aux_spec],
        out_specs=out_specs,
        out_shape=out_shape,
        compiler_params=_params("arbitrary", "arbitrary"),
    )(x3, cos_t, sin_t, *gains, w_lat, w_uq, w_uk, w_aux)


def _prompt_attn_kernel(q_ref, k_ref, vt_ref, o_ref, st0, st1, cm0, cm1, m_scr, l_scr, acc_scr, qt_scr, *, tq, vb,
                        hps):
    qi = pl.program_id(2)
    sts, cms = (st0, st1), (cm0, cm1)
    nv = tq // vb
    m_scr[...] = jnp.full(m_scr.shape, NEG_INF, F32)
    l_scr[...] = jnp.zeros(l_scr.shape, F32)
    acc_scr[...] = jnp.zeros(acc_scr.shape, F32)
    for hh in range(hps):
        qt_scr[hh] = q_ref[0, hh].astype(F32).T.astype(BF16)

    def q_phase(jk, slot, masked):
        for hh in range(hps):
            k = k_ref[0, hh, pl.ds(pl.multiple_of(jk * tq, tq), tq), :]
            st = _dot(k, qt_scr[hh])
            if masked:
                kpos = lax.broadcasted_iota(jnp.int32, (tq, tq), 0)
                qpos = lax.broadcasted_iota(jnp.int32, (tq, tq), 1)
                st = jnp.where(kpos <= qpos, st, NEG_INF)
            sts[slot][hh] = st
            cms[slot][hh] = jnp.max(st, axis=0, keepdims=True)

    def sp_phase(jv, slot):
        for hh in range(hps):
            m_prev = m_scr[hh]
            m_new = jnp.maximum(m_prev, cms[slot][hh])
            alpha = jnp.exp2(m_prev - m_new)
            p = jnp.exp2(sts[slot][hh] - m_new)
            l_scr[hh] = alpha * l_scr[hh] + jnp.sum(p, axis=0, keepdims=True)
            pb = p.astype(BF16)
            pv = _dot(vt_ref[0, hh, jv * nv], pb[:vb])
            for s in range(1, nv):
                pv += _dot(vt_ref[0, hh, jv * nv + s], pb[s * vb:(s + 1) * vb])
            acc_scr[hh] = alpha * acc_scr[hh] + pv
            m_scr[hh] = m_new

    q_phase(qi, 0, True)

    def two_visits(i, carry):
        t = 2 * i
        q_phase(t, 1, False)
        sp_phase(jnp.where(t == 0, qi, t - 1), 0)
        q_phase(t + 1, 0, False)
        sp_phase(t, 1)
        return carry

    lax.fori_loop(0, qi // 2, two_visits, 0)

    @pl.when(qi % 2 == 1)
    def _():
        t = qi - 1
        q_phase(t, 1, False)
        sp_phase(jnp.where(t == 0, qi, t - 1), 0)
        sp_phase(t, 1)

    @pl.when(qi % 2 == 0)
    def _():
        sp_phase(jnp.where(qi == 0, qi, qi - 1), 0)

    for hh in range(hps):
        o = acc_scr[hh] * (1.0 / l_scr[hh])
        o_ref[0, :, hh * V_DIM:(hh + 1) * V_DIM] = o.T.astype(BF16)


def _prompt_attn(qcat, kcat, vt, *, tq, heads_per_step):
    b, h, s, w = qcat.shape
    nvb, vb = vt.shape[2], vt.shape[4]
    hps = heads_per_step
    return pl.pallas_call(
        functools.partial(_prompt_attn_kernel, tq=tq, vb=vb, hps=hps),
        grid=(b, h // hps, s // tq),
        in_specs=[pl.BlockSpec((1, hps, tq, w), lambda bi, hi, qi: (bi, hi, qi, 0)),
                  pl.BlockSpec((1, hps, s, w), lambda bi, hi, qi: (bi, hi, 0, 0)),
                  pl.BlockSpec((1, hps, nvb, V_DIM, vb), lambda bi, hi, qi: (bi, hi, 0, 0, 0))],
        out_specs=pl.BlockSpec((1, tq, hps * V_DIM), lambda bi, hi, qi: (bi, qi, hi)),
        out_shape=jax.ShapeDtypeStruct((b, s, h * V_DIM), BF16),
        scratch_shapes=[pltpu.VMEM((hps, tq, tq), F32), pltpu.VMEM((hps, tq, tq), F32),
                        pltpu.VMEM((hps, 1, tq), F32), pltpu.VMEM((hps, 1, tq), F32),
                        pltpu.VMEM((hps, 1, tq), F32), pltpu.VMEM((hps, 1, tq), F32),
                        pltpu.VMEM((hps, V_DIM, tq), F32), pltpu.VMEM((hps, w, tq), BF16)],
        compiler_params=_params("arbitrary", "arbitrary", "arbitrary"),
    )(qcat, kcat, vt)


def _expand_heads(inv_t):
    k = inv_t.shape[1]
    return jnp.broadcast_to(inv_t[:, None, :], (N_HEADS, SUBLANES, k)).reshape(N_HEADS * SUBLANES, k)


def _decode_kernel(nchunk, gpages, pt_ref, qa_ref, qr_ref, qa_nx_ref, qr_nx_ref, cnew_ref, krnew_ref, gnew_ref,
                   cache_c, cache_kr, cache_inv, o_ref,
                   raw_c, raw_kr, raw_inv, cb0, cb1, s0, s1,
                   qa_scr, qr_scr, qa_nx_scr, qr_nx_scr, m_scr, l_scr, acc_scr, lsum_scr, sems):
    b = pl.program_id(0)
    nseq = pl.num_programs(0)
    rows = N_HEADS * SUBLANES
    cbs, ss = (cb0, cb1), (s0, s1)
    raw_bufs = (raw_c, raw_kr, raw_inv)
    caches = (cache_c, cache_kr, cache_inv)

    def start_chunk(seq, chunk, r):
        for p in range(gpages):
            page = pt_ref[seq, chunk * gpages + p]
            for a in range(3):
                pltpu.make_async_copy(caches[a].at[page], raw_bufs[a].at[r, p], sems.at[r, a]).start()

    def wait_chunk(r):
        for a in range(3):
            pltpu.make_async_copy(caches[a].at[pl.ds(0, gpages)], raw_bufs[a].at[r], sems.at[r, a]).wait()

    def load_queries(qa_in, qr_in, qa_out, qr_out):
        qa_out[...] = qa_in[0].reshape(rows, KV_RANK).astype(BF16)
        qr = qr_in[0].reshape(rows, LANES)
        qr_out[...] = (qr[:, :ROPE_DIM] + qr[:, ROPE_DIM:]).astype(BF16)

    load_queries(qa_ref, qr_ref, qa_scr, qr_scr)
    load_queries(qa_nx_ref, qr_nx_ref, qa_nx_scr, qr_nx_scr)
    m_scr[...] = jnp.full(m_scr.shape, NEG_INF, F32)
    l_scr[...] = jnp.zeros(l_scr.shape, F32)
    acc_scr[...] = jnp.zeros(acc_scr.shape, F32)

    unit = 2 * PAGE_SIZE
    nunit = gpages // 2

    def a_unit(r, slot, u, queries):
        qa_q, qr_q = queries
        p0, p1 = 2 * u, 2 * u + 1
        cf = jnp.concatenate([raw_c[r, p0], raw_c[r, p1]], axis=0)
        cb = cf.astype(BF16)
        cbt = cf.T.astype(BF16)
        cbs[slot][u * unit:(u + 1) * unit, :] = cb
        inv2 = jnp.concatenate([raw_inv[r, p0], raw_inv[r, p1]], axis=1)
        kr2 = jnp.concatenate([raw_kr[r, p0], raw_kr[r, p1]], axis=1).astype(BF16)
        s = _dot(qa_q[...], cbt) * _expand_heads(inv2) + _dot(qr_q[...], kr2)
        ss[slot][:, u * unit:(u + 1) * unit] = s

    def b_prelude(slot):
        m_prev = m_scr[...]
        m_new = jnp.maximum(m_prev, jnp.max(ss[slot][...], axis=-1, keepdims=True))
        alpha = jnp.exp(m_prev - m_new)
        acc_scr[...] = alpha * acc_scr[...]
        l_scr[...] = alpha * l_scr[...]
        lsum_scr[...] = jnp.zeros(lsum_scr.shape, F32)
        m_scr[...] = m_new

    def b_unit(slot, u):
        p = jnp.exp(ss[slot][:, u * unit:(u + 1) * unit] - m_scr[...])
        lsum_scr[...] += p[:, :LANES] + p[:, LANES:]
        acc_scr[...] += _dot(p.astype(BF16), cbs[slot][u * unit:(u + 1) * unit, :])

    def b_finish():
        l_scr[...] += jnp.sum(lsum_scr[...], axis=-1, keepdims=True)

    def refill(seq, chunk, r):
        nxt = chunk + RAW_SLOTS
        wraps = nxt >= nchunk
        seq2 = jnp.minimum(jnp.where(wraps, seq + 1, seq), nseq - 1)
        start_chunk(seq2, jnp.where(wraps, nxt - nchunk, nxt), r)

    def stage(chunk_a, slot_a, slot_b, next_seq=False):
        seq = b + 1 if next_seq else b
        queries = (qa_nx_scr, qr_nx_scr) if next_seq else (qa_scr, qr_scr)
        r = lax.rem(seq * nchunk + chunk_a, RAW_SLOTS)
        wait_chunk(r)
        for u in range(nunit):
            a_unit(r, slot_a, u, queries)
            if slot_b is not None:
                if u == 0:
                    b_prelude(slot_b)
                b_unit(slot_b, u)
        if slot_b is not None:
            b_finish()
        refill(seq, chunk_a, r)

    @pl.when(b == 0)
    def _():
        for g in range(RAW_SLOTS):
            start_chunk(0, g, g)
        stage(0, 0, None)

    def pair(cc, carry):
        stage(2 * cc + 1, 1, 0)
        stage(2 * cc + 2, 0, 1)
        return carry

    lax.fori_loop(0, (nchunk - 2) // 2, pair, 0)
    stage(nchunk - 1, 1, 0)
    stage(0, 0, 1, next_seq=True)

    @pl.when(b == nseq - 1)
    def _():
        for r in range(RAW_SLOTS):
            wait_chunk(r)

    def update(s, values_b):
        m_prev = m_scr[...]
        m_new = jnp.maximum(m_prev, jnp.max(s, axis=-1, keepdims=True))
        p = jnp.exp(s - m_new)
        alpha = jnp.exp(m_prev - m_new)
        l_scr[...] = alpha * l_scr[...] + jnp.sum(p, axis=-1, keepdims=True)
        acc_scr[...] = alpha * acc_scr[...] + _dot(p.astype(BF16), values_b)
        m_scr[...] = m_new

    t_new = cnew_ref.shape[1]
    pad = lambda a: jnp.concatenate([a, jnp.zeros((PAGE_SIZE - t_new, a.shape[1]), F32)], axis=0).astype(BF16)
    cn = pad(cnew_ref[0])
    s_new = _dot_nt(qa_scr[...], cn) * gnew_ref[0] + _dot_nt(qr_scr[...], pad(krnew_ref[0]))
    tok = lax.broadcasted_iota(jnp.int32, (rows, PAGE_SIZE), 0) % SUBLANES
    key = lax.broadcasted_iota(jnp.int32, (rows, PAGE_SIZE), 1)
    update(jnp.where(key <= tok, s_new, NEG_INF), cn)
    o_ref[0] = acc_scr[...] * (1.0 / l_scr[...])


def _decode_attn(page_table, qa, qr, c_new, kr_new, g_new, cache_c, cache_kr, cache_inv, *, pages_per_chunk):
    nseq, npages = page_table.shape
    t_new = c_new.shape[1]
    gpages = pages_per_chunk
    nchunk = npages // gpages
    assert t_new == SUBLANES and npages % gpages == 0 and nchunk % 2 == 0 and nchunk > RAW_SLOTS
    rows = N_HEADS * t_new
    keys = gpages * PAGE_SIZE
    any_spec = pl.BlockSpec(memory_space=pl.ANY)
    nxt = lambda b: jnp.minimum(b + 1, nseq - 1)
    in_specs = [pl.BlockSpec((1, N_HEADS, t_new, KV_RANK), lambda b, pt: (0, 0, b, 0)),
                pl.BlockSpec((1, N_HEADS, t_new, LANES), lambda b, pt: (0, 0, b, 0)),
                pl.BlockSpec((1, N_HEADS, t_new, KV_RANK), lambda b, pt: (0, 0, nxt(b), 0)),
                pl.BlockSpec((1, N_HEADS, t_new, LANES), lambda b, pt: (0, 0, nxt(b), 0)),
                pl.BlockSpec((1, t_new, KV_RANK), lambda b, pt: (b, 0, 0)),
                pl.BlockSpec((1, t_new, ROPE_DIM), lambda b, pt: (b, 0, 0)),
                pl.BlockSpec((1, rows, PAGE_SIZE), lambda b, pt: (b, 0, 0)),
                any_spec, any_spec, any_spec]
    two = lambda shape, dtype: [pltpu.VMEM(shape, dtype), pltpu.VMEM(shape, dtype)]
    grid_spec = pltpu.PrefetchScalarGridSpec(
        num_scalar_prefetch=1,
        grid=(nseq,),
        in_specs=in_specs,
        out_specs=pl.BlockSpec((1, rows, KV_RANK), lambda b, pt: (b, 0, 0)),
        scratch_shapes=[pltpu.VMEM((RAW_SLOTS, gpages, PAGE_SIZE, KV_RANK), F32),
                        pltpu.VMEM((RAW_SLOTS, gpages, ROPE_DIM, PAGE_SIZE), F32),
                        pltpu.VMEM((RAW_SLOTS, gpages, N_HEADS, PAGE_SIZE), F32)]
                       + two((keys, KV_RANK), BF16) + two((rows, keys), F32)
                       + [pltpu.VMEM((rows, KV_RANK), BF16), pltpu.VMEM((rows, ROPE_DIM), BF16),
                          pltpu.VMEM((rows, KV_RANK), BF16), pltpu.VMEM((rows, ROPE_DIM), BF16),
                          pltpu.VMEM((rows, 1), F32), pltpu.VMEM((rows, 1), F32), pltpu.VMEM((rows, KV_RANK), F32),
                          pltpu.VMEM((rows, LANES), F32), pltpu.SemaphoreType.DMA((RAW_SLOTS, 3))],
    )
    return pl.pallas_call(
        functools.partial(_decode_kernel, nchunk, gpages),
        grid_spec=grid_spec,
        out_shape=jax.ShapeDtypeStruct((nseq, rows, KV_RANK), F32),
        compiler_params=_params("arbitrary"),
    )(page_table, qa, qr, qa, qr, c_new, kr_new, g_new, cache_c, cache_kr, cache_inv)


def _uv_kernel(o_ref, w_ref, out_ref):
    nseq, _, t, r = o_ref.shape
    o = o_ref[...].reshape(nseq * t, r).astype(BF16)
    out_ref[...] = _dot(o, w_ref[0]).astype(BF16)


def _uv_proj(o_lat4, w_uv_h):
    nseq, h, t, r = o_lat4.shape
    return pl.pallas_call(
        _uv_kernel,
        grid=(h,),
        in_specs=[pl.BlockSpec((nseq, 1, t, r), lambda i: (0, i, 0, 0)),
                  pl.BlockSpec((1, r, V_DIM), lambda i: (i, 0, 0))],
        out_specs=pl.BlockSpec((nseq * t, V_DIM), lambda i: (0, i)),
        out_shape=jax.ShapeDtypeStruct((nseq * t, h * V_DIM), BF16),
        compiler_params=_params("arbitrary"),
    )(o_lat4, w_uv_h)


def _merge_kernel(short_seq, x_ref, attn_ref, b_ref, u_ref, prev_ref, ga_ref, gc_ref, wconv_ref, wo_ref,
                  gffn_ref, h_out, n2_out):
    u = u_ref[...]
    bm, c = u.shape
    if short_seq:
        nseq = bm // SUBLANES
        u3 = u.reshape(nseq, SUBLANES, c)
        tok = lax.broadcasted_iota(jnp.int32, (nseq, SUBLANES, c), 1)
        st = prev_ref[...]
        s0, s1 = st[:, 0:1, :], st[:, 1:2, :]
        u1 = jnp.where(tok == 0, s1, pltpu.roll(u3, 1, axis=1))
        u2 = jnp.where(tok == 0, s0, jnp.where(tok == 1, s1, pltpu.roll(u3, 2, axis=1)))
        u1 = u1.reshape(bm, c)
        u2 = u2.reshape(bm, c)
    else:
        first = pl.program_id(1) == 0
        halo = jnp.where(first, 0.0, prev_ref[0])
        row = lax.broadcasted_iota(jnp.int32, (bm, c), 0)
        h1, h2 = halo[SUBLANES - 1:SUBLANES, :], halo[SUBLANES - 2:SUBLANES - 1, :]
        u1 = jnp.where(row == 0, h1, pltpu.roll(u, 1, axis=0))
        u2 = jnp.where(row == 0, h2, jnp.where(row == 1, h1, pltpu.roll(u, 2, axis=0)))
    wc = wconv_ref[...]
    conv = wc[0:1, :] * u2 + wc[1:2, :] * u1 + wc[2:3, :] * u
    mixed = ga_ref[...].astype(F32) * attn_ref[...].astype(F32) + gc_ref[...].astype(F32) * (b_ref[...].astype(F32) * conv)
    h = x_ref[...] + _dot(mixed.astype(BF16), wo_ref[...])
    h_out[...] = h
    n2_out[...] = (_unit_rms(h) * gffn_ref[...]).astype(BF16)


def _merge(short_seq, x3, attn3, b3, u3, prev, ga3, gc3, w_conv, w_o, g_ffn, *, bm):
    g, r, d = x3.shape
    row = pl.BlockSpec((None, bm, d), lambda b, i: (b, i, 0))
    if short_seq:
        prev_spec = pl.BlockSpec((bm // SUBLANES, CONV_K - 1, d), lambda b, i: (i, 0, 0))
    else:
        per = bm // SUBLANES
        prev_spec = pl.BlockSpec((1, SUBLANES, d), lambda b, i: (b, jnp.maximum(i * per - 1, 0), 0))
    return pl.pallas_call(
        functools.partial(_merge_kernel, short_seq),
        grid=(g, r // bm),
        in_specs=[row, row, row, row, prev_spec, row, row,
                  pl.BlockSpec(w_conv.shape, lambda b, i: (0, 0)),
                  pl.BlockSpec(w_o.shape, lambda b, i: (0, 0), pipeline_mode=pl.Buffered(1)),
                  pl.BlockSpec(g_ffn.shape, lambda b, i: (0, 0))],
        out_specs=[row, row],
        out_shape=[jax.ShapeDtypeStruct((g, r, d), F32), jax.ShapeDtypeStruct((g, r, d), BF16)],
        compiler_params=_params("arbitrary", "arbitrary"),
    )(x3, attn3, b3, u3, prev, ga3, gc3, w_conv, w_o, g_ffn)


def _ffn_kernel(n2_ref, h_ref, wg_ref, wu_ref, wd_ref, y_ref):
    @pl.when(pl.program_id(1) == 0)
    def _():
        y_ref[...] = h_ref[...]

    n2 = n2_ref[...]
    act = (jax.nn.silu(_dot(n2, wg_ref[...])) * _dot(n2, wu_ref[...])).astype(BF16)
    y_ref[...] += _dot(act, wd_ref[...])


def _ffn(n2, h, w_gate_up, w_down, *, bm, bf):
    m, d = h.shape
    d_ff = w_down.shape[0]
    nf = d_ff // bf
    row = pl.BlockSpec((bm, d), lambda i, f: (i, 0))
    return pl.pallas_call(
        _ffn_kernel,
        grid=(m // bm, nf),
        in_specs=[row,
                  pl.BlockSpec((bm, d), lambda i, f: (i, 0), pipeline_mode=pl.Buffered(1)),
                  pl.BlockSpec((d, bf), lambda i, f: (0, f)),
                  pl.BlockSpec((d, bf), lambda i, f: (0, f + nf)),
                  pl.BlockSpec((bf, d), lambda i, f: (f, 0))],
        out_specs=row,
        out_shape=jax.ShapeDtypeStruct((m, d), F32),
        compiler_params=_params("arbitrary", "arbitrary"),
    )(n2, h, w_gate_up, w_gate_up, w_down)


def _rope_tables(pos):
    freqs = ROPE_THETA ** (-jnp.arange(HALF_ROPE, dtype=F32) / HALF_ROPE)
    ang = pos.astype(F32)[:, None] * freqs
    c, s = jnp.cos(ang), jnp.sin(ang)
    return jnp.concatenate([c, c, c, c], axis=-1), jnp.concatenate([-s, s, -s, s], axis=-1)


def _swap_halves(a):
    return jnp.concatenate([a[..., HALF_ROPE:], a[..., :HALF_ROPE]], axis=-1)


def kernel(x_prompt, x_sample, cache_kv_latent, cache_k_rope, cache_k_inv_rms, state_conv, page_table,
           g_attn, w_in, b_gate, g_q_lat, g_kv_lat, w_uq, g_q_nope, g_q_rope, g_k_nope, g_k_rope,
           w_uk, w_uv, w_conv, w_o, g_ffn, w_gate_up, w_down):
    batch, seq, d = x_prompt.shape
    nseq, t_dec, _ = x_sample.shape
    past_len = page_table.shape[1] * PAGE_SIZE
    width = w_conv.shape[1]
    d_ff = w_down.shape[0]

    i1 = Q_RANK + KV_RANK
    i2 = i1 + ROPE_DIM
    i3, i4, i5 = i2 + width, i2 + 2 * width, i2 + 3 * width
    w_lat = jnp.concatenate([w_in[:, :i2], _swap_halves(w_in[:, i1:i2])], axis=1).astype(BF16)
    w_b, w_c, w_x = (w_in[:, a:a + width].astype(BF16) for a in (i2, i3, i4))
    w_ga, w_gc = w_in[:, i5:i5 + d].astype(BF16), w_in[:, i5 + d:].astype(BF16)
    b_ga, b_gc = b_gate[:d].reshape(1, d), b_gate[d:].reshape(1, d)
    uq_rope = w_uq[:, :, NOPE_DIM:]
    w_uq_all = jnp.concatenate([w_uq[:, :, :NOPE_DIM].reshape(Q_RANK, -1), uq_rope.reshape(Q_RANK, -1),
                                _swap_halves(uq_rope).reshape(Q_RANK, -1)], axis=1).astype(BF16)
    w_uk2 = w_uk.reshape(KV_RANK, -1).astype(BF16)
    w_uk_t = jnp.transpose(w_uk, (1, 2, 0)).astype(BF16)
    w_uv_t = jnp.transpose(w_uv, (1, 2, 0)).reshape(N_HEADS * V_DIM, KV_RANK).astype(BF16)
    w_uv_h = jnp.transpose(w_uv, (1, 0, 2)).astype(BF16)
    w_o_b, w_gu_b, w_down_b = w_o.astype(BF16), w_gate_up.astype(BF16), w_down.astype(BF16)
    row = lambda a: a.reshape(1, -1)
    g_kr = row(jnp.concatenate([g_k_rope, _swap_halves(g_k_rope)]))
    g_qr_a = row(jnp.concatenate([g_q_rope, g_q_rope]))
    g_qr_b = row(jnp.concatenate([_swap_halves(g_q_rope)] * 2))
    gains = [row(g_attn), row(g_q_lat), row(g_kv_lat), g_kr, row(g_q_nope), g_qr_a, g_qr_b, row(g_k_nope)]

    bm_lat = 256
    cos_p, sin_p = _rope_tables(jnp.arange(seq))
    cos_s, sin_s = _rope_tables(jnp.tile(past_len + jnp.arange(t_dec), bm_lat // t_dec))

    xp2 = x_prompt.reshape(batch * seq, d)
    b_p, u_p, ga_p, gc_p = _rest_proj(xp2, row(g_attn), w_b, w_c, w_x, w_ga, w_gc, b_ga, b_gc, bm=1024, bn=256)
    qcat, kcat, vt, ckv_p, krope_p, kinv_p = _latent_proj(
        True, x_prompt, cos_p, sin_p, gains, w_lat, w_uq_all, w_uk2, w_uv_t, bm=bm_lat)
    attn_p = _prompt_attn(qcat, kcat, vt, tq=512, heads_per_step=2)
    to3 = lambda a: a.reshape(batch, seq, -1)
    u_p3 = to3(u_p)
    h_p, n2_p = _merge(False, x_prompt, attn_p, to3(b_p), u_p3, u_p3, to3(ga_p), to3(gc_p), w_conv, w_o_b,
                       row(g_ffn), bm=256)
    y_prompt = _ffn(n2_p.reshape(-1, d), h_p.reshape(-1, d), w_gu_b, w_down_b, bm=1024, bf=512).reshape(batch, seq, d)
    conv_state_prompt = u_p3[:, seq - (CONV_K - 1):, :]

    m_s = nseq * t_dec
    xs2 = x_sample.reshape(m_s, d)
    b_s, u_s, ga_s, gc_s = _rest_proj(xs2, row(g_attn), w_b, w_c, w_x, w_ga, w_gc, b_ga, b_gc, bm=1024, bn=256)
    qa, qr, ckv_s, krope_s, kinv_s = _latent_proj(
        False, xs2.reshape(1, m_s, d), cos_s, sin_s, gains, w_lat, w_uq_all, w_uk2, w_uk_t, bm=bm_lat)
    ckv_s = ckv_s.reshape(nseq, t_dec, KV_RANK)
    krope_s = krope_s.reshape(nseq, t_dec, ROPE_DIM)
    kinv_s = kinv_s.reshape(nseq, t_dec, N_HEADS)
    g_new = jnp.repeat(jnp.swapaxes(kinv_s, 1, 2), t_dec, axis=1)
    g_new = jnp.concatenate([g_new, jnp.ones((nseq, N_HEADS * t_dec, PAGE_SIZE - t_dec), F32)], axis=-1)
    cache_kr_t = jnp.swapaxes(cache_k_rope, 1, 2)
    cache_inv_t = jnp.swapaxes(cache_k_inv_rms, 1, 2)
    o_lat = _decode_attn(page_table, qa, qr, ckv_s, krope_s, g_new, cache_kv_latent, cache_kr_t, cache_inv_t,
                         pages_per_chunk=16)
    attn_s = _uv_proj(o_lat.reshape(nseq, N_HEADS, t_dec, KV_RANK), w_uv_h)
    to1 = lambda a: a.reshape(1, m_s, -1)
    h_s, n2_s = _merge(True, to1(xs2), to1(attn_s), to1(b_s), to1(u_s), state_conv.astype(F32), to1(ga_s),
                       to1(gc_s), w_conv, w_o_b, row(g_ffn), bm=256)
    y_sample = _ffn(n2_s.reshape(-1, d), h_s.reshape(-1, d), w_gu_b, w_down_b, bm=1024, bf=512).reshape(nseq, t_dec, d)
    u_s3 = u_s.reshape(nseq, t_dec, width)
    conv_state_sample = u_s3[:, t_dec - (CONV_K - 1):, :]

    return (y_prompt, y_sample, ckv_p, krope_p, kinv_p, conv_state_prompt,
            ckv_s, krope_s, kinv_s, conv_state_sample)
```

```python
import functools

import jax
import jax.numpy as jnp
from jax import lax
from jax.experimental import pallas as pl
from jax.experimental.pallas import tpu as pltpu

N_HEADS = 16
Q_RANK = 512
KV_RANK = 512
NOPE_DIM = 128
ROPE_DIM = 64
HALF_ROPE = ROPE_DIM // 2
V_DIM = 128
QK_DIM = NOPE_DIM + ROPE_DIM
QK_PAD = 256
ROPE_THETA = 10000.0
ATTN_SCALE = QK_DIM ** -0.5
LOG2_E = 1.4426950408889634
PAGE_SIZE = 128
RAW_SLOTS = 3
CONV_K = 3
EPS = 1e-6
NEG_INF = -1e30
LANES = 128
SUBLANES = 8
VMEM_LIMIT = 56 * 1024 * 1024

F32 = jnp.float32
BF16 = jnp.bfloat16
NT_DIMS = (((1,), (1,)), ((), ()))


def _params(*sem):
    return pltpu.CompilerParams(dimension_semantics=sem, vmem_limit_bytes=VMEM_LIMIT)


def _unit_rms(xf):
    return xf * lax.rsqrt(jnp.mean(xf * xf, axis=-1, keepdims=True) + EPS)


def _dot(a, b):
    return jnp.dot(a, b, preferred_element_type=F32)


def _dot_nt(a, b):
    return lax.dot_general(a, b, NT_DIMS, preferred_element_type=F32)


def _rest_kernel(x_ref, g_ref, wb_ref, wc_ref, wx_ref, wga_ref, wgc_ref, bga_ref, bgc_ref,
                 b_out, u_out, ga_out, gc_out, n_scr):
    @pl.when(pl.program_id(1) == 0)
    def _():
        n_scr[...] = (_unit_rms(x_ref[...]) * g_ref[...]).astype(BF16)

    n = n_scr[...]
    d_b = _dot(n, wb_ref[...])
    d_c = _dot(n, wc_ref[...])
    b_out[...] = d_b.astype(BF16)
    d_x = _dot(n, wx_ref[...])
    d_ga = _dot(n, wga_ref[...])
    u_out[...] = d_c * d_x
    d_gc = _dot(n, wgc_ref[...])
    ga_out[...] = jax.nn.sigmoid(d_ga + bga_ref[...]).astype(BF16)
    gc_out[...] = jax.nn.sigmoid(d_gc + bgc_ref[...]).astype(BF16)


def _rest_proj(x2, g_attn, w_b, w_c, w_x, w_ga, w_gc, b_ga, b_gc, *, bm, bn):
    m, d = x2.shape
    n = w_b.shape[1]
    row = lambda i, j: (i, 0)
    col = lambda i, j: (0, j)
    tile = lambda i, j: (i, j)
    w_spec = pl.BlockSpec((d, bn), col)
    v_spec = pl.BlockSpec((1, bn), col)
    o_spec = pl.BlockSpec((bm, bn), tile)
    return pl.pallas_call(
        _rest_kernel,
        grid=(m // bm, n // bn),
        in_specs=[pl.BlockSpec((bm, d), row), pl.BlockSpec((1, d), lambda i, j: (0, 0)),
                  w_spec, w_spec, w_spec, w_spec, w_spec, v_spec, v_spec],
        out_specs=[o_spec, o_spec, o_spec, o_spec],
        out_shape=[jax.ShapeDtypeStruct((m, n), BF16), jax.ShapeDtypeStruct((m, n), F32),
                   jax.ShapeDtypeStruct((m, n), BF16), jax.ShapeDtypeStruct((m, n), BF16)],
        scratch_shapes=[pltpu.VMEM((bm, d), BF16)],
        compiler_params=_params("arbitrary", "arbitrary"),
    )(x2, g_attn, w_b, w_c, w_x, w_ga, w_gc, b_ga, b_gc)


def _latent_kernel(prompt, x_ref, cos_ref, sin_ref, g_attn, g_q_lat, g_kv_lat, g_kr, g_q_nope, g_qr_a,
                   g_qr_b, g_k_nope, w_lat, w_uq, w_uk, w_aux, *outs):
    if prompt:
        qcat_out, kcat_out, vt_out, ckv_out, krope_out, kinv_out = outs
    else:
        qa_out, qr_out, ckv_out, krope_out, kinv_out = outs
    bm = x_ref.shape[1]
    cos_t = cos_ref[...]
    sin_t = sin_ref[...]
    lane = lax.broadcasted_iota(jnp.int32, (bm, LANES), 1)
    low_half = lane < ROPE_DIM
    q_scale = ATTN_SCALE * LOG2_E if prompt else ATTN_SCALE

    n = (_unit_rms(x_ref[0]) * g_attn[...]).astype(BF16)
    lat = _dot(n, w_lat[...])
    cq = (_unit_rms(lat[:, :Q_RANK]) * g_q_lat[...]).astype(BF16)
    ckv = _unit_rms(lat[:, Q_RANK:Q_RANK + KV_RANK]) * g_kv_lat[...]
    ckv_out[0] = ckv
    ckv_b = ckv.astype(BF16)

    krp = lat[:, Q_RANK + KV_RANK:]
    kr_ms = jnp.sum(jnp.where(low_half, krp * krp, 0.0), axis=-1, keepdims=True) * (1.0 / ROPE_DIM)
    krp = krp * lax.rsqrt(kr_ms + EPS) * g_kr[...]
    krt = krp * jnp.where(low_half, cos_t, sin_t)
    kr2 = krt + pltpu.roll(krt, ROPE_DIM, axis=1)
    krope_out[0] = kr2[:, :ROPE_DIM]

    kraw = _dot(ckv_b, w_uk[...])
    kinv = jnp.zeros((bm, LANES), F32)
    for h in range(N_HEADS):
        kh = kraw[:, h * NOPE_DIM:(h + 1) * NOPE_DIM]
        inv_h = lax.rsqrt(jnp.mean(kh * kh, axis=-1, keepdims=True) + EPS)
        kinv = jnp.where(lane == h, inv_h, kinv)
        if prompt:
            kcat_out[0, h, :, :NOPE_DIM] = (kh * inv_h * g_k_nope[...]).astype(BF16)
            kcat_out[0, h, :, NOPE_DIM:] = kr2.astype(BF16)
    kinv_out[0] = kinv[:, :N_HEADS]

    if prompt:
        vt = _dot_nt(w_aux[...], ckv_b)
        vt_out[0, :, 0] = vt.reshape(N_HEADS, V_DIM, bm).astype(BF16)

    q = _dot(cq, w_uq[...])
    rope_a0 = N_HEADS * NOPE_DIM
    rope_b0 = rope_a0 + N_HEADS * ROPE_DIM
    for h in range(N_HEADS):
        qh = _unit_rms(q[:, h * NOPE_DIM:(h + 1) * NOPE_DIM]) * (g_q_nope[...] * q_scale)
        if prompt:
            qcat_out[0, h, :, :NOPE_DIM] = qh.astype(BF16)
        else:
            qa_out[0, h] = _dot((qh * g_k_nope[...]).astype(BF16), w_aux[h])
    for p in range(N_HEADS // 2):
        qa_ = q[:, rope_a0 + p * LANES: rope_a0 + (p + 1) * LANES]
        qb_ = q[:, rope_b0 + p * LANES: rope_b0 + (p + 1) * LANES]
        sq = qa_ * qa_
        ms_lo = jnp.sum(jnp.where(low_half, sq, 0.0), axis=-1, keepdims=True) * (1.0 / ROPE_DIM)
        ms_hi = jnp.sum(jnp.where(low_half, 0.0, sq), axis=-1, keepdims=True) * (1.0 / ROPE_DIM)
        inv = jnp.where(low_half, lax.rsqrt(ms_lo + EPS), lax.rsqrt(ms_hi + EPS)) * q_scale
        rot = (qa_ * g_qr_a[...] * cos_t + qb_ * g_qr_b[...] * sin_t) * inv
        even = jnp.where(low_half, rot, 0.0)
        odd = jnp.where(low_half, 0.0, rot)
        if prompt:
            qcat_out[0, 2 * p, :, NOPE_DIM:] = even.astype(BF16)
            qcat_out[0, 2 * p + 1, :, NOPE_DIM:] = odd.astype(BF16)
        else:
            qr_out[0, 2 * p] = even
            qr_out[0, 2 * p + 1] = odd


def _latent_proj(prompt, x3, cos_t, sin_t, gains, w_lat, w_uq, w_uk, w_aux, *, bm):
    g, r, d = x3.shape
    nblk = r // bm
    tab_blocks = cos_t.shape[0] // bm
    const2 = lambda a: pl.BlockSpec(a.shape, lambda b, i: (0, 0), pipeline_mode=pl.Buffered(1))
    const3 = lambda a: pl.BlockSpec(a.shape, lambda b, i: (0, 0, 0), pipeline_mode=pl.Buffered(1))
    tab_spec = pl.BlockSpec((bm, LANES), lambda b, i: (i % tab_blocks, 0))
    row3 = lambda w: pl.BlockSpec((1, bm, w), lambda b, i: (b, i, 0))
    head4 = lambda w: pl.BlockSpec((1, N_HEADS, bm, w), lambda b, i: (b, 0, i, 0))
    if prompt:
        out_specs = [head4(QK_PAD), head4(QK_PAD),
                     pl.BlockSpec((1, N_HEADS, 1, V_DIM, bm), lambda b, i: (b, 0, i, 0, 0)),
                     row3(KV_RANK), row3(ROPE_DIM), row3(N_HEADS)]
        out_shape = [jax.ShapeDtypeStruct((g, N_HEADS, r, QK_PAD), BF16),
                     jax.ShapeDtypeStruct((g, N_HEADS, r, QK_PAD), BF16),
                     jax.ShapeDtypeStruct((g, N_HEADS, nblk, V_DIM, bm), BF16)]
    else:
        out_specs = [head4(KV_RANK), head4(LANES), row3(KV_RANK), row3(ROPE_DIM), row3(N_HEADS)]
        out_shape = [jax.ShapeDtypeStruct((g, N_HEADS, r, KV_RANK), F32),
                     jax.ShapeDtypeStruct((g, N_HEADS, r, LANES), F32)]
    out_shape += [jax.ShapeDtypeStruct((g, r, KV_RANK), F32), jax.ShapeDtypeStruct((g, r, ROPE_DIM), F32),
                  jax.ShapeDtypeStruct((g, r, N_HEADS), F32)]
    aux_spec = const2(w_aux) if prompt else const3(w_aux)
    return pl.pallas_call(
        functools.partial(_latent_kernel, prompt),
        grid=(g, nblk),
        in_specs=[pl.BlockSpec((1, bm, d), lambda b, i: (b, i, 0)), tab_spec, tab_spec]
                 + [const2(a) for a in gains] + [const2(w_lat), const2(w_uq), const2(w_uk), aux_spec],
        out_specs=out_specs,
        out_shape=out_shape,
        compiler_params=_params("arbitrary", "arbitrary"),
    )(x3, cos_t, sin_t, *gains, w_lat, w_uq, w_uk, w_aux)


def _prompt_attn_kernel(q_ref, k_ref, vt_ref, o_ref, st0, st1, cm0, cm1, m_scr, l_scr, acc_scr, qt_scr, *, tq, vb,
                        hps):
    qi = pl.program_id(2)
    sts, cms = (st0, st1), (cm0, cm1)
    nv = tq // vb
    m_scr[...] = jnp.full(m_scr.shape, NEG_INF, F32)
    l_scr[...] = jnp.zeros(l_scr.shape, F32)
    acc_scr[...] = jnp.zeros(acc_scr.shape, F32)
    for hh in range(hps):
        qt_scr[hh] = q_ref[0, hh].astype(F32).T.astype(BF16)

    def q_phase(jk, slot, masked):
        for hh in range(hps):
            k = k_ref[0, hh, pl.ds(pl.multiple_of(jk * tq, tq), tq), :]
            st = _dot(k, qt_scr[hh])
            if masked:
                kpos = lax.broadcasted_iota(jnp.int32, (tq, tq), 0)
                qpos = lax.broadcasted_iota(jnp.int32, (tq, tq), 1)
                st = jnp.where(kpos <= qpos, st, NEG_INF)
            sts[slot][hh] = st
            cms[slot][hh] = jnp.max(st, axis=0, keepdims=True)

    def sp_phase(jv, slot):
        for hh in range(hps):
            m_prev = m_scr[hh]
            m_new = jnp.maximum(m_prev, cms[slot][hh])
            alpha = jnp.exp2(m_prev - m_new)
            p = jnp.exp2(sts[slot][hh] - m_new)
            l_scr[hh] = alpha * l_scr[hh] + jnp.sum(p, axis=0, keepdims=True)
            pb = p.astype(BF16)
            pv = _dot(vt_ref[0, hh, jv * nv], pb[:vb])
            for s in range(1, nv):
                pv += _dot(vt_ref[0, hh, jv * nv + s], pb[s * vb:(s + 1) * vb])
            acc_scr[hh] = alpha * acc_scr[hh] + pv
            m_scr[hh] = m_new

    q_phase(qi, 0, True)

    def two_visits(i, carry):
        t = 2 * i
        q_phase(t, 1, False)
        sp_phase(jnp.where(t == 0, qi, t - 1), 0)
        q_phase(t + 1, 0, False)
        sp_phase(t, 1)
        return carry

    lax.fori_loop(0, qi // 2, two_visits, 0)

    @pl.when(qi % 2 == 1)
    def _():
        t = qi - 1
        q_phase(t, 1, False)
        sp_phase(jnp.where(t == 0, qi, t - 1), 0)
        sp_phase(t, 1)

    @pl.when(qi % 2 == 0)
    def _():
        sp_phase(jnp.where(qi == 0, qi, qi - 1), 0)

    for hh in range(hps):
        o = acc_scr[hh] * (1.0 / l_scr[hh])
        o_ref[0, :, hh * V_DIM:(hh + 1) * V_DIM] = o.T.astype(BF16)


def _prompt_attn(qcat, kcat, vt, *, tq, heads_per_step):
    b, h, s, w = qcat.shape
    nvb, vb = vt.shape[2], vt.shape[4]
    hps = heads_per_step
    return pl.pallas_call(
        functools.partial(_prompt_attn_kernel, tq=tq, vb=vb, hps=hps),
        grid=(b, h // hps, s // tq),
        in_specs=[pl.BlockSpec((1, hps, tq, w), lambda bi, hi, qi: (bi, hi, qi, 0)),
                  pl.BlockSpec((1, hps, s, w), lambda bi, hi, qi: (bi, hi, 0, 0)),
                  pl.BlockSpec((1, hps, nvb, V_DIM, vb), lambda bi, hi, qi: (bi, hi, 0, 0, 0))],
        out_specs=pl.BlockSpec((1, tq, hps * V_DIM), lambda bi, hi, qi: (bi, qi, hi)),
        out_shape=jax.ShapeDtypeStruct((b, s, h * V_DIM), BF16),
        scratch_shapes=[pltpu.VMEM((hps, tq, tq), F32), pltpu.VMEM((hps, tq, tq), F32),
                        pltpu.VMEM((hps, 1, tq), F32), pltpu.VMEM((hps, 1, tq), F32),
                        pltpu.VMEM((hps, 1, tq), F32), pltpu.VMEM((hps, 1, tq), F32),
                        pltpu.VMEM((hps, V_DIM, tq), F32), pltpu.VMEM((hps, w, tq), BF16)],
        compiler_params=_params("arbitrary", "arbitrary", "arbitrary"),
    )(qcat, kcat, vt)


def _expand_heads(inv_t):
    k = inv_t.shape[1]
    return jnp.broadcast_to(inv_t[:, None, :], (N_HEADS, SUBLANES, k)).reshape(N_HEADS * SUBLANES, k)


def _decode_kernel(nchunk, gpages, pt_ref, qa_ref, qr_ref, qa_nx_ref, qr_nx_ref, cnew_ref, krnew_ref, gnew_ref,
                   cache_c, cache_kr, cache_inv, o_ref,
                   raw_c, raw_kr, raw_inv, cb0, cb1, s0, s1,
                   qa_scr, qr_scr, qa_nx_scr, qr_nx_scr, m_scr, l_scr, acc_scr, lsum_scr, sems):
    b = pl.program_id(0)
    nseq = pl.num_programs(0)
    rows = N_HEADS * SUBLANES
    cbs, ss = (cb0, cb1), (s0, s1)
    raw_bufs = (raw_c, raw_kr, raw_inv)
    caches = (cache_c, cache_kr, cache_inv)

    def start_chunk(seq, chunk, r):
        for p in range(gpages):
            page = pt_ref[seq, chunk * gpages + p]
            for a in range(3):
                pltpu.make_async_copy(caches[a].at[page], raw_bufs[a].at[r, p], sems.at[r, a]).start()

    def wait_chunk(r):
        for a in range(3):
            pltpu.make_async_copy(caches[a].at[pl.ds(0, gpages)], raw_bufs[a].at[r], sems.at[r, a]).wait()

    def load_queries(qa_in, qr_in, qa_out, qr_out):
        qa_out[...] = qa_in[0].reshape(rows, KV_RANK).astype(BF16)
        qr = qr_in[0].reshape(rows, LANES)
        qr_out[...] = (qr[:, :ROPE_DIM] + qr[:, ROPE_DIM:]).astype(BF16)

    load_queries(qa_ref, qr_ref, qa_scr, qr_scr)
    load_queries(qa_nx_ref, qr_nx_ref, qa_nx_scr, qr_nx_scr)
    m_scr[...] = jnp.full(m_scr.shape, NEG_INF, F32)
    l_scr[...] = jnp.zeros(l_scr.shape, F32)
    acc_scr[...] = jnp.zeros(acc_scr.shape, F32)

    unit = 2 * PAGE_SIZE
    nunit = gpages // 2

    def a_unit(r, slot, u, queries):
        qa_q, qr_q = queries
        p0, p1 = 2 * u, 2 * u + 1
        cf = jnp.concatenate([raw_c[r, p0], raw_c[r, p1]], axis=0)
        cb = cf.astype(BF16)
        cbt = cf.T.astype(BF16)
        cbs[slot][u * unit:(u + 1) * unit, :] = cb
        inv2 = jnp.concatenate([raw_inv[r, p0], raw_inv[r, p1]], axis=1)
        kr2 = jnp.concatenate([raw_kr[r, p0], raw_kr[r, p1]], axis=1).astype(BF16)
        s = _dot(qa_q[...], cbt) * _expand_heads(inv2) + _dot(qr_q[...], kr2)
        ss[slot][:, u * unit:(u + 1) * unit] = s

    def b_prelude(slot):
        m_prev = m_scr[...]
        m_new = jnp.maximum(m_prev, jnp.max(ss[slot][...], axis=-1, keepdims=True))
        alpha = jnp.exp(m_prev - m_new)
        acc_scr[...] = alpha * acc_scr[...]
        l_scr[...] = alpha * l_scr[...]
        lsum_scr[...] = jnp.zeros(lsum_scr.shape, F32)
        m_scr[...] = m_new

    def b_unit(slot, u):
        p = jnp.exp(ss[slot][:, u * unit:(u + 1) * unit] - m_scr[...])
        lsum_scr[...] += p[:, :LANES] + p[:, LANES:]
        acc_scr[...] += _dot(p.astype(BF16), cbs[slot][u * unit:(u + 1) * unit, :])

    def b_finish():
        l_scr[...] += jnp.sum(lsum_scr[...], axis=-1, keepdims=True)

    def refill(seq, chunk, r):
        nxt = chunk + RAW_SLOTS
        wraps = nxt >= nchunk
        seq2 = jnp.minimum(jnp.where(wraps, seq + 1, seq), nseq - 1)
        start_chunk(seq2, jnp.where(wraps, nxt - nchunk, nxt), r)

    def stage(chunk_a, slot_a, slot_b, next_seq=False):
        seq = b + 1 if next_seq else b
        queries = (qa_nx_scr, qr_nx_scr) if next_seq else (qa_scr, qr_scr)
        r = lax.rem(seq * nchunk + chunk_a, RAW_SLOTS)
        wait_chunk(r)
        for u in range(nunit):
            a_unit(r, slot_a, u, queries)
            if slot_b is not None:
                if u == 0:
                    b_prelude(slot_b)
                b_unit(slot_b, u)
        if slot_b is not None:
            b_finish()
        refill(seq, chunk_a, r)

    @pl.when(b == 0)
    def _():
        for g in range(RAW_SLOTS):
            start_chunk(0, g, g)
        stage(0, 0, None)

    def pair(cc, carry):
        stage(2 * cc + 1, 1, 0)
        stage(2 * cc + 2, 0, 1)
        return carry

    lax.fori_loop(0, (nchunk - 2) // 2, pair, 0)
    stage(nchunk - 1, 1, 0)
    stage(0, 0, 1, next_seq=True)

    @pl.when(b == nseq - 1)
    def _():
        for r in range(RAW_SLOTS):
            wait_chunk(r)

    def update(s, values_b):
        m_prev = m_scr[...]
        m_new = jnp.maximum(m_prev, jnp.max(s, axis=-1, keepdims=True))
        p = jnp.exp(s - m_new)
        alpha = jnp.exp(m_prev - m_new)
        l_scr[...] = alpha * l_scr[...] + jnp.sum(p, axis=-1, keepdims=True)
        acc_scr[...] = alpha * acc_scr[...] + _dot(p.astype(BF16), values_b)
        m_scr[...] = m_new

    t_new = cnew_ref.shape[1]
    pad = lambda a: jnp.concatenate([a, jnp.zeros((PAGE_SIZE - t_new, a.shape[1]), F32)], axis=0).astype(BF16)
    cn = pad(cnew_ref[0])
    s_new = _dot_nt(qa_scr[...], cn) * gnew_ref[0] + _dot_nt(qr_scr[...], pad(krnew_ref[0]))
    tok = lax.broadcasted_iota(jnp.int32, (rows, PAGE_SIZE), 0) % SUBLANES
    key = lax.broadcasted_iota(jnp.int32, (rows, PAGE_SIZE), 1)
    update(jnp.where(key <= tok, s_new, NEG_INF), cn)
    o_ref[0] = acc_scr[...] * (1.0 / l_scr[...])


def _decode_attn(page_table, qa, qr, c_new, kr_new, g_new, cache_c, cache_kr, cache_inv, *, pages_per_chunk):
    nseq, npages = page_table.shape
    t_new = c_new.shape[1]
    gpages = pages_per_chunk
    nchunk = npages // gpages
    assert t_new == SUBLANES and npages % gpages == 0 and nchunk % 2 == 0 and nchunk > RAW_SLOTS
    rows = N_HEADS * t_new
    keys = gpages * PAGE_SIZE
    any_spec = pl.BlockSpec(memory_space=pl.ANY)
    nxt = lambda b: jnp.minimum(b + 1, nseq - 1)
    in_specs = [pl.BlockSpec((1, N_HEADS, t_new, KV_RANK), lambda b, pt: (0, 0, b, 0)),
                pl.BlockSpec((1, N_HEADS, t_new, LANES), lambda b, pt: (0, 0, b, 0)),
                pl.BlockSpec((1, N_HEADS, t_new, KV_RANK), lambda b, pt: (0, 0, nxt(b), 0)),
                pl.BlockSpec((1, N_HEADS, t_new, LANES), lambda b, pt: (0, 0, nxt(b), 0)),
                pl.BlockSpec((1, t_new, KV_RANK), lambda b, pt: (b, 0, 0)),
                pl.BlockSpec((1, t_new, ROPE_DIM), lambda b, pt: (b, 0, 0)),
                pl.BlockSpec((1, rows, PAGE_SIZE), lambda b, pt: (b, 0, 0)),
                any_spec, any_spec, any_spec]
    two = lambda shape, dtype: [pltpu.VMEM(shape, dtype), pltpu.VMEM(shape, dtype)]
    grid_spec = pltpu.PrefetchScalarGridSpec(
        num_scalar_prefetch=1,
        grid=(nseq,),
        in_specs=in_specs,
        out_specs=pl.BlockSpec((1, rows, KV_RANK), lambda b, pt: (b, 0, 0)),
        scratch_shapes=[pltpu.VMEM((RAW_SLOTS, gpages, PAGE_SIZE, KV_RANK), F32),
                        pltpu.VMEM((RAW_SLOTS, gpages, ROPE_DIM, PAGE_SIZE), F32),
                        pltpu.VMEM((RAW_SLOTS, gpages, N_HEADS, PAGE_SIZE), F32)]
                       + two((keys, KV_RANK), BF16) + two((rows, keys), F32)
                       + [pltpu.VMEM((rows, KV_RANK), BF16), pltpu.VMEM((rows, ROPE_DIM), BF16),
                          pltpu.VMEM((rows, KV_RANK), BF16), pltpu.VMEM((rows, ROPE_DIM), BF16),
                          pltpu.VMEM((rows, 1), F32), pltpu.VMEM((rows, 1), F32), pltpu.VMEM((rows, KV_RANK), F32),
                          pltpu.VMEM((rows, LANES), F32), pltpu.SemaphoreType.DMA((RAW_SLOTS, 3))],
    )
    return pl.pallas_call(
        functools.partial(_decode_kernel, nchunk, gpages),
        grid_spec=grid_spec,
        out_shape=jax.ShapeDtypeStruct((nseq, rows, KV_RANK), F32),
        compiler_params=_params("arbitrary"),
    )(page_table, qa, qr, qa, qr, c_new, kr_new, g_new, cache_c, cache_kr, cache_inv)


def _uv_kernel(o_ref, w_ref, out_ref):
    nseq, _, t, r = o_ref.shape
    o = o_ref[...].reshape(nseq * t, r).astype(BF16)
    out_ref[...] = _dot(o, w_ref[0]).astype(BF16)


def _uv_proj(o_lat4, w_uv_h):
    nseq, h, t, r = o_lat4.shape
    return pl.pallas_call(
        _uv_kernel,
        grid=(h,),
        in_specs=[pl.BlockSpec((nseq, 1, t, r), lambda i: (0, i, 0, 0)),
                  pl.BlockSpec((1, r, V_DIM), lambda i: (i, 0, 0))],
        out_specs=pl.BlockSpec((nseq * t, V_DIM), lambda i: (0, i)),
        out_shape=jax.ShapeDtypeStruct((nseq * t, h * V_DIM), BF16),
        compiler_params=_params("arbitrary"),
    )(o_lat4, w_uv_h)


def _merge_kernel(short_seq, x_ref, attn_ref, b_ref, u_ref, prev_ref, ga_ref, gc_ref, wconv_ref, wo_ref,
                  gffn_ref, h_out, n2_out):
    u = u_ref[...]
    bm, c = u.shape
    if short_seq:
        nseq = bm // SUBLANES
        u3 = u.reshape(nseq, SUBLANES, c)
        tok = lax.broadcasted_iota(jnp.int32, (nseq, SUBLANES, c), 1)
        st = prev_ref[...]
        s0, s1 = st[:, 0:1, :], st[:, 1:2, :]
        u1 = jnp.where(tok == 0, s1, pltpu.roll(u3, 1, axis=1))
        u2 = jnp.where(tok == 0, s0, jnp.where(tok == 1, s1, pltpu.roll(u3, 2, axis=1)))
        u1 = u1.reshape(bm, c)
        u2 = u2.reshape(bm, c)
    else:
        first = pl.program_id(1) == 0
        halo = jnp.where(first, 0.0, prev_ref[0])
        row = lax.broadcasted_iota(jnp.int32, (bm, c), 0)
        h1, h2 = halo[SUBLANES - 1:SUBLANES, :], halo[SUBLANES - 2:SUBLANES - 1, :]
        u1 = jnp.where(row == 0, h1, pltpu.roll(u, 1, axis=0))
        u2 = jnp.where(row == 0, h2, jnp.where(row == 1, h1, pltpu.roll(u, 2, axis=0)))
    wc = wconv_ref[...]
    conv = wc[0:1, :] * u2 + wc[1:2, :] * u1 + wc[2:3, :] * u
    mixed = ga_ref[...].astype(F32) * attn_ref[...].astype(F32) + gc_ref[...].astype(F32) * (b_ref[...].astype(F32) * conv)
    h = x_ref[...] + _dot(mixed.astype(BF16), wo_ref[...])
    h_out[...] = h
    n2_out[...] = (_unit_rms(h) * gffn_ref[...]).astype(BF16)


def _merge(short_seq, x3, attn3, b3, u3, prev, ga3, gc3, w_conv, w_o, g_ffn, *, bm):
    g, r, d = x3.shape
    row = pl.BlockSpec((None, bm, d), lambda b, i: (b, i, 0))
    if short_seq:
        prev_spec = pl.BlockSpec((bm // SUBLANES, CONV_K - 1, d), lambda b, i: (i, 0, 0))
    else:
        per = bm // SUBLANES
        prev_spec = pl.BlockSpec((1, SUBLANES, d), lambda b, i: (b, jnp.maximum(i * per - 1, 0), 0))
    return pl.pallas_call(
        functools.partial(_merge_kernel, short_seq),
        grid=(g, r // bm),
        in_specs=[row, row, row, row, prev_spec, row, row,
                  pl.BlockSpec(w_conv.shape, lambda b, i: (0, 0)),
                  pl.BlockSpec(w_o.shape, lambda b, i: (0, 0), pipeline_mode=pl.Buffered(1)),
                  pl.BlockSpec(g_ffn.shape, lambda b, i: (0, 0))],
        out_specs=[row, row],
        out_shape=[jax.ShapeDtypeStruct((g, r, d), F32), jax.ShapeDtypeStruct((g, r, d), BF16)],
        compiler_params=_params("arbitrary", "arbitrary"),
    )(x3, attn3, b3, u3, prev, ga3, gc3, w_conv, w_o, g_ffn)


def _ffn_kernel(n2_ref, h_ref, wg_ref, wu_ref, wd_ref, y_ref):
    @pl.when(pl.program_id(1) == 0)
    def _():
        y_ref[...] = h_ref[...]

    n2 = n2_ref[...]
    act = (jax.nn.silu(_dot(n2, wg_ref[...])) * _dot(n2, wu_ref[...])).astype(BF16)
    y_ref[...] += _dot(act, wd_ref[...])


def _ffn(n2, h, w_gate_up, w_down, *, bm, bf):
    m, d = h.shape
    d_ff = w_down.shape[0]
    nf = d_ff // bf
    row = pl.BlockSpec((bm, d), lambda i, f: (i, 0))
    return pl.pallas_call(
        _ffn_kernel,
        grid=(m // bm, nf),
        in_specs=[row,
                  pl.BlockSpec((bm, d), lambda i, f: (i, 0), pipeline_mode=pl.Buffered(1)),
                  pl.BlockSpec((d, bf), lambda i, f: (0, f)),
                  pl.BlockSpec((d, bf), lambda i, f: (0, f + nf)),
                  pl.BlockSpec((bf, d), lambda i, f: (f, 0))],
        out_specs=row,
        out_shape=jax.ShapeDtypeStruct((m, d), F32),
        compiler_params=_params("arbitrary", "arbitrary"),
    )(n2, h, w_gate_up, w_gate_up, w_down)


def _rope_tables(pos):
    freqs = ROPE_THETA ** (-jnp.arange(HALF_ROPE, dtype=F32) / HALF_ROPE)
    ang = pos.astype(F32)[:, None] * freqs
    c, s = jnp.cos(ang), jnp.sin(ang)
    return jnp.concatenate([c, c, c, c], axis=-1), jnp.concatenate([-s, s, -s, s], axis=-1)


def _swap_halves(a):
    return jnp.concatenate([a[..., HALF_ROPE:], a[..., :HALF_ROPE]], axis=-1)


def kernel(x_prompt, x_sample, cache_kv_latent, cache_k_rope, cache_k_inv_rms, state_conv, page_table,
           g_attn, w_in, b_gate, g_q_lat, g_kv_lat, w_uq, g_q_nope, g_q_rope, g_k_nope, g_k_rope,
           w_uk, w_uv, w_conv, w_o, g_ffn, w_gate_up, w_down):
    batch, seq, d = x_prompt.shape
    nseq, t_dec, _ = x_sample.shape
    past_len = page_table.shape[1] * PAGE_SIZE
    width = w_conv.shape[1]
    d_ff = w_down.shape[0]

    i1 = Q_RANK + KV_RANK
    i2 = i1 + ROPE_DIM
    i3, i4, i5 = i2 + width, i2 + 2 * width, i2 + 3 * width
    w_lat = jnp.concatenate([w_in[:, :i2], _swap_halves(w_in[:, i1:i2])], axis=1).astype(BF16)
    w_b, w_c, w_x = (w_in[:, a:a + width].astype(BF16) for a in (i2, i3, i4))
    w_ga, w_gc = w_in[:, i5:i5 + d].astype(BF16), w_in[:, i5 + d:].astype(BF16)
    b_ga, b_gc = b_gate[:d].reshape(1, d), b_gate[d:].reshape(1, d)
    uq_rope = w_uq[:, :, NOPE_DIM:]
    w_uq_all = jnp.concatenate([w_uq[:, :, :NOPE_DIM].reshape(Q_RANK, -1), uq_rope.reshape(Q_RANK, -1),
                                _swap_halves(uq_rope).reshape(Q_RANK, -1)], axis=1).astype(BF16)
    w_uk2 = w_uk.reshape(KV_RANK, -1).astype(BF16)
    w_uk_t = jnp.transpose(w_uk, (1, 2, 0)).astype(BF16)
    w_uv_t = jnp.transpose(w_uv, (1, 2, 0)).reshape(N_HEADS * V_DIM, KV_RANK).astype(BF16)
    w_uv_h = jnp.transpose(w_uv, (1, 0, 2)).astype(BF16)
    w_o_b, w_gu_b, w_down_b = w_o.astype(BF16), w_gate_up.astype(BF16), w_down.astype(BF16)
    row = lambda a: a.reshape(1, -1)
    g_kr = row(jnp.concatenate([g_k_rope, _swap_halves(g_k_rope)]))
    g_qr_a = row(jnp.concatenate([g_q_rope, g_q_rope]))
    g_qr_b = row(jnp.concatenate([_swap_halves(g_q_rope)] * 2))
    gains = [row(g_attn), row(g_q_lat), row(g_kv_lat), g_kr, row(g_q_nope), g_qr_a, g_qr_b, row(g_k_nope)]

    bm_lat = 256
    cos_p, sin_p = _rope_tables(jnp.arange(seq))
    cos_s, sin_s = _rope_tables(jnp.tile(past_len + jnp.arange(t_dec), bm_lat // t_dec))

    xp2 = x_prompt.reshape(batch * seq, d)
    b_p, u_p, ga_p, gc_p = _rest_proj(xp2, row(g_attn), w_b, w_c, w_x, w_ga, w_gc, b_ga, b_gc, bm=1024, bn=256)
    qcat, kcat, vt, ckv_p, krope_p, kinv_p = _latent_proj(
        True, x_prompt, cos_p, sin_p, gains, w_lat, w_uq_all, w_uk2, w_uv_t, bm=bm_lat)
    attn_p = _prompt_attn(qcat, kcat, vt, tq=512, heads_per_step=4)
    to3 = lambda a: a.reshape(batch, seq, -1)
    u_p3 = to3(u_p)
    h_p, n2_p = _merge(False, x_prompt, attn_p, to3(b_p), u_p3, u_p3, to3(ga_p), to3(gc_p), w_conv, w_o_b,
                       row(g_ffn), bm=256)
    y_prompt = _ffn(n2_p.reshape(-1, d), h_p.reshape(-1, d), w_gu_b, w_down_b, bm=1024, bf=512).reshape(batch, seq, d)
    conv_state_prompt = u_p3[:, seq - (CONV_K - 1):, :]

    m_s = nseq * t_dec
    xs2 = x_sample.reshape(m_s, d)
    b_s, u_s, ga_s, gc_s = _rest_proj(xs2, row(g_attn), w_b, w_c, w_x, w_ga, w_gc, b_ga, b_gc, bm=1024, bn=256)
    qa, qr, ckv_s, krope_s, kinv_s = _latent_proj(
        False, xs2.reshape(1, m_s, d), cos_s, sin_s, gains, w_lat, w_uq_all, w_uk2, w_uk_t, bm=bm_lat)
    ckv_s = ckv_s.reshape(nseq, t_dec, KV_RANK)
    krope_s = krope_s.reshape(nseq, t_dec, ROPE_DIM)
    kinv_s = kinv_s.reshape(nseq, t_dec, N_HEADS)
    g_new = jnp.repeat(jnp.swapaxes(kinv_s, 1, 2), t_dec, axis=1)
    g_new = jnp.concatenate([g_new, jnp.ones((nseq, N_HEADS * t_dec, PAGE_SIZE - t_dec), F32)], axis=-1)
    cache_kr_t = jnp.swapaxes(cache_k_rope, 1, 2)
    cache_inv_t = jnp.swapaxes(cache_k_inv_rms, 1, 2)
    o_lat = _decode_attn(page_table, qa, qr, ckv_s, krope_s, g_new, cache_kv_latent, cache_kr_t, cache_inv_t,
                         pages_per_chunk=16)
    attn_s = _uv_proj(o_lat.reshape(nseq, N_HEADS, t_dec, KV_RANK), w_uv_h)
    to1 = lambda a: a.reshape(1, m_s, -1)
    h_s, n2_s = _merge(True, to1(xs2), to1(attn_s), to1(b_s), to1(u_s), state_conv.astype(F32), to1(ga_s),
                       to1(gc_s), w_conv, w_o_b, row(g_ffn), bm=256)
    y_sample = _ffn(n2_s.reshape(-1, d), h_s.reshape(-1, d), w_gu_b, w_down_b, bm=1024, bf=512).reshape(nseq, t_dec, d)
    u_s3 = u_s.reshape(nseq, t_dec, width)
    conv_state_sample = u_s3[:, t_dec - (CONV_K - 1):, :]

    return (y_prompt, y_sample, ckv_p, krope_p, kinv_p, conv_state_prompt,
            ckv_s, krope_s, kinv_s, conv_state_sample)
```
